```python
import math
import jax, jax.numpy as jnp
from jax import lax
import numpy as np

D_MODEL = 1024
BATCH = 4
SEQ = 4096
DEPTH = 1
DEC_BATCH = 128
DEC_SEQ = 4
PAST_LEN = 8192
PAGE_SIZE = 128

MLA_HEADS = D_MODEL // 128
MLA_NOPE = 128
MLA_ROPE = 64
MLA_V = 128
Q_LORA = 256
KV_LORA = 256
ROPE_THETA = 10000.0
MLA_Q_BLOCK = 128
MLA_SCALE = (MLA_NOPE + MLA_ROPE) ** -0.5
MOBA_HEADS = D_MODEL // 128
MOBA_HEAD_DIM = 128
MOBA_WIDTH = MOBA_HEADS * MOBA_HEAD_DIM
MOBA_BLOCK = 256
MOBA_TOPK = 3
MOBA_Q_CHUNK = 16
MOBA_SCALE = MOBA_HEAD_DIM ** -0.5
T5_BUCKETS = 32
T5_MAX_DIST = 128
D_FF = ((8 * D_MODEL + 3 * 256 - 1) // (3 * 256)) * 256
IN_SPLITS = [Q_LORA,
             Q_LORA + KV_LORA,
             Q_LORA + KV_LORA + MLA_ROPE,
             Q_LORA + KV_LORA + MLA_ROPE + MOBA_WIDTH,
             Q_LORA + KV_LORA + MLA_ROPE + 2 * MOBA_WIDTH,
             Q_LORA + KV_LORA + MLA_ROPE + 3 * MOBA_WIDTH,
             Q_LORA + KV_LORA + MLA_ROPE + 3 * MOBA_WIDTH + D_MODEL]
D_IN = Q_LORA + KV_LORA + MLA_ROPE + 3 * MOBA_WIDTH + 2 * D_MODEL
EPS = 1e-6
NEG = -1e30

kernel_name = "mla_moba_gated_hybrid_step"


def rms_norm(x, g):
    xf = x.astype(jnp.float32)
    y = xf * lax.rsqrt(jnp.mean(xf * xf, axis=-1, keepdims=True) + EPS)
    return (y * g.astype(jnp.float32)).astype(x.dtype)


def rope(x, pos):
    half = x.shape[-1] // 2
    freqs = ROPE_THETA ** (-jnp.arange(half, dtype=jnp.float32) / half)
    ang = pos.astype(jnp.float32)[..., None] * freqs
    cos, sin = jnp.cos(ang), jnp.sin(ang)
    xf = x.astype(jnp.float32)
    x1, x2 = xf[..., :half], xf[..., half:]
    return jnp.concatenate([x1 * cos - x2 * sin, x2 * cos + x1 * sin], axis=-1).astype(x.dtype)


def t5_bucket(dist):
    max_exact = T5_BUCKETS // 2
    d = jnp.maximum(dist, max_exact).astype(jnp.float32)
    large = max_exact + (jnp.log(d / max_exact) / math.log(T5_MAX_DIST / max_exact)
                         * (T5_BUCKETS - max_exact)).astype(jnp.int32)
    return jnp.where(dist < max_exact, dist, jnp.minimum(large, T5_BUCKETS - 1))


def branch_inputs(x, pos, norm_attn, w_in, q_a_norm, w_q_b, kv_a_norm,
                  mla_qn_gain, mla_qr_gain, mla_kr_gain, moba_q_gain, moba_k_gain):
    lead = x.shape[:-1]
    xn = rms_norm(x, norm_attn)
    q_a, c_raw, kr_raw, mq, mk, mv, g_a, g_b = jnp.split(xn @ w_in, IN_SPLITS, axis=-1)
    q = (rms_norm(q_a, q_a_norm) @ w_q_b).reshape(*lead, MLA_HEADS, MLA_NOPE + MLA_ROPE)
    q_nope = rms_norm(q[..., :MLA_NOPE], mla_qn_gain)
    q_pe = rope(rms_norm(q[..., MLA_NOPE:], mla_qr_gain), pos[:, None])
    c = rms_norm(c_raw, kv_a_norm)
    k_pe = rope(rms_norm(kr_raw, mla_kr_gain), pos)
    qm = rms_norm(mq.reshape(*lead, MOBA_HEADS, MOBA_HEAD_DIM), moba_q_gain)
    km = rms_norm(mk.reshape(*lead, MOBA_HEADS, MOBA_HEAD_DIM), moba_k_gain)
    vm = mv.reshape(*lead, MOBA_HEADS, MOBA_HEAD_DIM)
    return q_nope, q_pe, c, k_pe, qm, km, vm, g_a, g_b


def mla_expand(c, w_kv_b, kn_gain):
    kv = (c @ w_kv_b).reshape(*c.shape[:-1], MLA_HEADS, MLA_NOPE + MLA_V)
    return rms_norm(kv[..., :MLA_NOPE], kn_gain), kv[..., MLA_NOPE:]


def mla_prompt(q_nope, q_pe, c, k_pe, w_kv_b, kn_gain):
    B, S, H, _ = q_nope.shape
    k_nope, v = mla_expand(c, w_kv_b, kn_gain)
    nq = S // MLA_Q_BLOCK
    blocks = lambda a: a.reshape(B, nq, MLA_Q_BLOCK, *a.shape[2:]).swapaxes(0, 1)
    kpos = jnp.arange(S)

    def one_block(args):
        qn, qp, qpos = args
        s = (jnp.einsum('bqhd,bkhd->bhqk', qn, k_nope)
             + jnp.einsum('bqhr,bkr->bhqk', qp, k_pe)).astype(jnp.float32) * MLA_SCALE
        s = jnp.where(kpos[None, :] <= qpos[:, None], s, NEG)
        p = jax.nn.softmax(s, axis=-1).astype(v.dtype)
        return jnp.einsum('bhqk,bkhd->bqhd', p, v)

    out = lax.map(one_block, (blocks(q_nope), blocks(q_pe), kpos.reshape(nq, MLA_Q_BLOCK)))
    return out.swapaxes(0, 1).reshape(B, S, H * MLA_V)


def mla_sample(q_nope, q_pe, c_new, kpe_new, cache_c, cache_kpe, layer, page_table, w_kv_b, kn_gain):
    DB, Q, H, _ = q_nope.shape
    past = page_table.shape[1] * PAGE_SIZE
    mask = jnp.arange(past + Q)[None, :] <= (past + jnp.arange(Q))[:, None]

    def one_seq(args):
        qn, qp, cn, kn, pages = args
        c_all = jnp.concatenate([cache_c[layer, pages].reshape(past, KV_LORA), cn], axis=0)
        kpe_all = jnp.concatenate([cache_kpe[layer, pages].reshape(past, MLA_ROPE), kn], axis=0)
        k_nope, v = mla_expand(c_all, w_kv_b, kn_gain)
        s = (jnp.einsum('qhd,khd->hqk', qn, k_nope)
             + jnp.einsum('qhr,kr->hqk', qp, kpe_all)).astype(jnp.float32) * MLA_SCALE
        s = jnp.where(mask, s, NEG)
        p = jax.nn.softmax(s, axis=-1).astype(v.dtype)
        return jnp.einsum('hqk,khd->qhd', p, v).reshape(Q, H * MLA_V)

    return lax.map(one_seq, (q_nope, q_pe, c_new, kpe_new, page_table))


def moba_prompt(q, k, v, t5_table):
    B, S, H, D = q.shape
    nb = -(-S // MOBA_BLOCK)
    pad = nb * MOBA_BLOCK - S
    k_blocks = jnp.pad(k, ((0, 0), (0, pad), (0, 0), (0, 0))).reshape(B, nb, MOBA_BLOCK, H, D)
    v_blocks = jnp.pad(v, ((0, 0), (0, pad), (0, 0), (0, 0))).reshape(B, nb, MOBA_BLOCK, H, D)
    pos = jnp.arange(S)
    own = pos // MOBA_BLOCK
    own_b = jnp.broadcast_to(own[None, :, None, None], (B, S, H, 1))
    n_sel = min(MOBA_TOPK, nb - 1)
    if n_sel > 0:
        means = jnp.mean(k_blocks.astype(jnp.float32), axis=2)
        gate = jnp.einsum('bshd,bnhd->bshn', q.astype(jnp.float32), means)
        gate = jnp.where(jnp.arange(nb)[None, None, None, :] < own[None, :, None, None], gate, NEG)
        _, top = lax.top_k(gate, n_sel)
        blocks = jnp.concatenate([top, own_b], axis=-1)
        valid = jnp.concatenate([top < own_b, jnp.ones((B, S, H, 1), bool)], axis=-1)
    else:
        blocks, valid = own_b, jnp.ones((B, S, H, 1), bool)
    n_chunks = S // MOBA_Q_CHUNK
    chunks = lambda a: a.reshape(B, n_chunks, MOBA_Q_CHUNK, *a.shape[2:]).swapaxes(0, 1)
    b_idx = jnp.arange(B)[:, None, None, None]
    h_idx = jnp.arange(H)[None, None, :, None]

    def one_chunk(args):
        qc, blk, bval, qpos = args
        ks = k_blocks[b_idx, blk, :, h_idx]
        vs = v_blocks[b_idx, blk, :, h_idx]
        logits = jnp.einsum('bqhd,bqhnkd->bqhnk', qc, ks).astype(jnp.float32) * MOBA_SCALE
        kpos = blk[..., None] * MOBA_BLOCK + jnp.arange(MOBA_BLOCK)
        dist = qpos[None, :, None, None, None] - kpos
        bias = t5_table[t5_bucket(jnp.maximum(dist, 0)), h_idx[..., None]].astype(jnp.float32)
        logits = jnp.where(bval[..., None] & (dist >= 0), logits + bias, NEG)
        p = jax.nn.softmax(logits.reshape(*logits.shape[:3], -1), axis=-1)
        p = p.reshape(logits.shape).astype(vs.dtype)
        return jnp.einsum('bqhnk,bqhnkd->bqhd', p, vs)

    out = lax.map(one_chunk, (chunks(q), chunks(blocks), chunks(valid), pos.reshape(n_chunks, MOBA_Q_CHUNK)))
    return out.swapaxes(0, 1).reshape(B, S, H * D)


def moba_sample(q, k_new, v_new, cache_k, cache_v, layer, page_table, t5_table):
    DB, Q, H, D = q.shape
    n_pages = page_table.shape[1]
    past = n_pages * PAGE_SIZE
    ppb = MOBA_BLOCK // PAGE_SIZE
    n_full = past // MOBA_BLOCK
    n_sel = min(MOBA_TOPK, n_full)
    own_first_page = n_full * ppb
    qpos = past + jnp.arange(Q)
    h_idx = jnp.arange(H)
    xs = (q, k_new, v_new, page_table)
    if n_sel > 0:
        page_sums = jnp.sum(cache_k, axis=2, dtype=jnp.float32)[layer]
        means = page_sums[page_table[:, :n_full * ppb]].reshape(DB, n_full, ppb, H, D).sum(2) / MOBA_BLOCK
        xs = xs + (means,)

    def one_seq(args):
        qb, kn, vn, pages = args[:4]
        k_parts, v_parts, pos_parts = [], [], []
        if n_sel > 0:
            gate = jnp.einsum('qhd,nhd->qhn', qb.astype(jnp.float32), args[4])
            _, top = lax.top_k(gate, n_sel)
            phys = pages[top[..., None] * ppb + jnp.arange(ppb)]
            hh = h_idx[None, :, None, None]
            k_parts.append(cache_k[layer, phys, :, hh].reshape(Q, H, n_sel * MOBA_BLOCK, D))
            v_parts.append(cache_v[layer, phys, :, hh].reshape(Q, H, n_sel * MOBA_BLOCK, D))
            pos_parts.append((top[..., None] * MOBA_BLOCK + jnp.arange(MOBA_BLOCK)).reshape(Q, H, -1))
        if own_first_page < n_pages:
            own = pages[own_first_page:]
            n_own = (n_pages - own_first_page) * PAGE_SIZE
            ko = cache_k[layer, own].reshape(n_own, H, D).swapaxes(0, 1)
            vo = cache_v[layer, own].reshape(n_own, H, D).swapaxes(0, 1)
            k_parts.append(jnp.broadcast_to(ko[None], (Q, H, n_own, D)))
            v_parts.append(jnp.broadcast_to(vo[None], (Q, H, n_own, D)))
            pos_parts.append(jnp.broadcast_to(n_full * MOBA_BLOCK + jnp.arange(n_own), (Q, H, n_own)))
        k_parts.append(jnp.broadcast_to(kn.swapaxes(0, 1)[None], (Q, H, Q, D)))
        v_parts.append(jnp.broadcast_to(vn.swapaxes(0, 1)[None], (Q, H, Q, D)))
        pos_parts.append(jnp.broadcast_to(qpos, (Q, H, Q)))
        ks = jnp.concatenate(k_parts, axis=2)
        vs = jnp.concatenate(v_parts, axis=2)
        kpos = jnp.concatenate(pos_parts, axis=2)
        logits = jnp.einsum('qhd,qhkd->qhk', qb, ks).astype(jnp.float32) * MOBA_SCALE
        dist = qpos[:, None, None] - kpos
        bias = t5_table[t5_bucket(jnp.maximum(dist, 0)), h_idx[None, :, None]].astype(jnp.float32)
        logits = jnp.where(dist >= 0, logits + bias, NEG)
        p = jax.nn.softmax(logits, axis=-1).astype(vs.dtype)
        return jnp.einsum('qhk,qhkd->qhd', p, vs).reshape(Q, H * D)

    return lax.map(one_seq, xs)


def merge_and_ffn(x, o_a, o_b, g_a, g_b, w_o, norm_ffn, w_gate_up, w_down):
    h = x + (jax.nn.sigmoid(g_a) * o_a + jax.nn.sigmoid(g_b) * o_b) @ w_o
    gate, up = jnp.split(rms_norm(h, norm_ffn) @ w_gate_up, 2, axis=-1)
    return h + (jax.nn.silu(gate) * up) @ w_down


def setup_inputs(seed: int = 0) -> dict:
    key = jax.random.key(seed)
    ks = jax.random.split(key, 24)
    n_pages = PAST_LEN // PAGE_SIZE
    n_pool = (DEC_BATCH * n_pages * 5) // 4
    nrm = lambda k, shape, scale=1.0: jax.random.normal(k, shape, jnp.float32) * scale
    gain = lambda k, shape: 1.0 + 0.02 * jax.random.normal(k, shape, jnp.float32)
    page_table = jax.random.permutation(ks[6], n_pool)[:DEC_BATCH * n_pages]
    page_table = page_table.reshape(DEC_BATCH, n_pages).astype(jnp.int32)
    return {
        "x_prompt": nrm(ks[0], (BATCH, SEQ, D_MODEL)),
        "x_sample": nrm(ks[1], (DEC_BATCH, DEC_SEQ, D_MODEL)),
        "cache_mla_latent": nrm(ks[2], (DEPTH, n_pool, PAGE_SIZE, KV_LORA)),
        "cache_mla_kpe": nrm(ks[3], (DEPTH, n_pool, PAGE_SIZE, MLA_ROPE)),
        "cache_moba_k": nrm(ks[4], (DEPTH, n_pool, PAGE_SIZE, MOBA_HEADS, MOBA_HEAD_DIM)),
        "cache_moba_v": nrm(ks[5], (DEPTH, n_pool, PAGE_SIZE, MOBA_HEADS, MOBA_HEAD_DIM)),
        "page_table": page_table,
        "norm_attn": gain(ks[7], (DEPTH, D_MODEL)),
        "w_in": nrm(ks[8], (DEPTH, D_MODEL, D_IN), D_MODEL ** -0.5),
        "q_a_norm": gain(ks[9], (DEPTH, Q_LORA)),
        "w_q_b": nrm(ks[10], (DEPTH, Q_LORA, MLA_HEADS * (MLA_NOPE + MLA_ROPE)), Q_LORA ** -0.5),
        "kv_a_norm": gain(ks[11], (DEPTH, KV_LORA)),
        "w_kv_b": nrm(ks[12], (DEPTH, KV_LORA, MLA_HEADS * (MLA_NOPE + MLA_V)), KV_LORA ** -0.5),
        "mla_qn_gain": gain(ks[13], (DEPTH, MLA_NOPE)),
        "mla_qr_gain": gain(ks[14], (DEPTH, MLA_ROPE)),
        "mla_kn_gain": gain(ks[15], (DEPTH, MLA_NOPE)),
        "mla_kr_gain": gain(ks[16], (DEPTH, MLA_ROPE)),
        "moba_q_gain": gain(ks[17], (DEPTH, MOBA_HEAD_DIM)),
        "moba_k_gain": gain(ks[18], (DEPTH, MOBA_HEAD_DIM)),
        "t5_table": nrm(ks[19], (T5_BUCKETS, MOBA_HEADS), 0.5),
        "w_o": nrm(ks[20], (DEPTH, D_MODEL, D_MODEL), D_MODEL ** -0.5),
        "norm_ffn": gain(ks[21], (DEPTH, D_MODEL)),
        "w_gate_up": nrm(ks[22], (DEPTH, D_MODEL, 2 * D_FF), D_MODEL ** -0.5),
        "w_down": nrm(ks[23], (DEPTH, D_FF, D_MODEL), D_FF ** -0.5),
    }


def reference(x_prompt, x_sample, cache_mla_latent, cache_mla_kpe, cache_moba_k, cache_moba_v,
              page_table, norm_attn, w_in, q_a_norm, w_q_b, kv_a_norm, w_kv_b,
              mla_qn_gain, mla_qr_gain, mla_kn_gain, mla_kr_gain, moba_q_gain, moba_k_gain,
              t5_table, w_o, norm_ffn, w_gate_up, w_down):
    pos_p = jnp.arange(x_prompt.shape[1])
    past = page_table.shape[1] * PAGE_SIZE
    pos_s = past + jnp.arange(x_sample.shape[1])
    yp, ys = x_prompt, x_sample
    lat_p, kpe_p, k_p, v_p = [], [], [], []
    lat_s, kpe_s, k_s, v_s = [], [], [], []
    for l in range(DEPTH):
        lw = (norm_attn[l], w_in[l], q_a_norm[l], w_q_b[l], kv_a_norm[l],
              mla_qn_gain[l], mla_qr_gain[l], mla_kr_gain[l], moba_q_gain[l], moba_k_gain[l])
        ffw = (w_o[l], norm_ffn[l], w_gate_up[l], w_down[l])
        q_nope, q_pe, c, k_pe, qm, km, vm, g_a, g_b = branch_inputs(yp, pos_p, *lw)
        o_a = mla_prompt(q_nope, q_pe, c, k_pe, w_kv_b[l], mla_kn_gain[l])
        o_b = moba_prompt(qm, km, vm, t5_table)
        lat_p.append(c); kpe_p.append(k_pe); k_p.append(km); v_p.append(vm)
        yp = merge_and_ffn(yp, o_a, o_b, g_a, g_b, *ffw)
        q_nope, q_pe, c, k_pe, qm, km, vm, g_a, g_b = branch_inputs(ys, pos_s, *lw)
        o_a = mla_sample(q_nope, q_pe, c, k_pe, cache_mla_latent, cache_mla_kpe, l, page_table,
                         w_kv_b[l], mla_kn_gain[l])
        o_b = moba_sample(qm, km, vm, cache_moba_k, cache_moba_v, l, page_table, t5_table)
        lat_s.append(c); kpe_s.append(k_pe); k_s.append(km); v_s.append(vm)
        ys = merge_and_ffn(ys, o_a, o_b, g_a, g_b, *ffw)
    new_latent_prompt = jnp.stack(lat_p, axis=0)
    new_kpe_prompt = jnp.stack(kpe_p, axis=0)
    new_k_prompt = jnp.stack(k_p, axis=0)
    new_v_prompt = jnp.stack(v_p, axis=0)
    new_latent_sample = jnp.stack(lat_s, axis=0)
    new_kpe_sample = jnp.stack(kpe_s, axis=0)
    new_k_sample = jnp.stack(k_s, axis=0)
    new_v_sample = jnp.stack(v_s, axis=0)
    return (yp, ys, new_latent_prompt, new_kpe_prompt, new_k_prompt, new_v_prompt,
            new_latent_sample, new_kpe_sample, new_k_sample, new_v_sample)
```

```python
import functools
import math

import numpy as np
import jax
import jax.numpy as jnp
from jax import lax
from jax.experimental import pallas as pl
from jax.experimental.pallas import tpu as pltpu

F32 = jnp.float32
BF16 = jnp.bfloat16

D_MODEL = 1024
HEADS = 8
HEAD_DIM = 128
ROPE_DIM = 64
Q_LORA = 256
KV_LORA = 256
QK_PAD = 256
ROPE_THETA = 10000.0
MLA_SCALE = (HEAD_DIM + ROPE_DIM) ** -0.5
MOBA_BLOCK = 256
MOBA_TOPK = 3
MOBA_SCALE = HEAD_DIM ** -0.5
PAGE_SIZE = 128
PAGES_PER_BLOCK = MOBA_BLOCK // PAGE_SIZE
T5_BUCKETS = 32
T5_MAX_DIST = 128
EPS = 1e-6
NEG = -1e30
ROW_TILE = 256
MLA_TILE = 512
VMEM_LIMIT = 56 * 1024 * 1024

_NT = (((1,), (1,)), ((), ()))


def _dot(a, b):
    return jnp.dot(a, b, preferred_element_type=F32)


def _dot_nt(a, b):
    return lax.dot_general(a, b, _NT, preferred_element_type=F32)


def _split(x):
    hi = x.astype(BF16)
    lo = (x - hi.astype(F32)).astype(BF16)
    return hi, lo


def _rms(x, g):
    ms = jnp.mean(x * x, axis=-1, keepdims=True)
    return x * lax.rsqrt(ms + EPS) * g


def _rope(x, cosf, sinf):
    half = ROPE_DIM // 2
    swapped = jnp.concatenate([x[:, half:], x[:, :half]], axis=-1)
    return x * cosf + swapped * sinf


def _const_spec(shape):
    n = len(shape)
    return pl.BlockSpec(shape, lambda *_: (0,) * n, pipeline_mode=pl.Buffered(1))


def _params(sem):
    return pltpu.CompilerParams(dimension_semantics=sem, vmem_limit_bytes=VMEM_LIMIT)


def _t5_bucket_upper_bounds(max_dist):
    d = np.arange(max_dist + 1)
    max_exact = T5_BUCKETS // 2
    dd = np.maximum(d, max_exact).astype(np.float32)
    large = max_exact + (np.log(dd / np.float32(max_exact)) / np.float32(math.log(T5_MAX_DIST / max_exact))
                         * np.float32(T5_BUCKETS - max_exact)).astype(np.int32)
    bucket = np.where(d < max_exact, d, np.minimum(large, T5_BUCKETS - 1))
    assert bucket[-1] == T5_BUCKETS - 1
    return [int(d[bucket == b].max()) if np.any(bucket == b) else None for b in range(T5_BUCKETS)]


def _t5_bias(dist, h, t5_ref, bounds):
    res = jnp.full(dist.shape, t5_ref[T5_BUCKETS - 1, h], F32)
    for b in range(T5_BUCKETS - 2, -1, -1):
        if bounds[b] is not None:
            res = jnp.where(dist <= bounds[b], t5_ref[b, h], res)
    return res


def _t5_kernel(t5_ref, tiles_ref, last_ref, new_ref, *, bounds, dec_seq):
    h = pl.program_id(0)
    r = lax.broadcasted_iota(jnp.int32, (MOBA_BLOCK, MOBA_BLOCK), 0)
    c = lax.broadcasted_iota(jnp.int32, (MOBA_BLOCK, MOBA_BLOCK), 1)
    tiles_ref[0] = _t5_bias(r - c, h, t5_ref, bounds)
    tiles_ref[1] = _t5_bias(MOBA_BLOCK + r - c, h, t5_ref, bounds)
    q = lax.broadcasted_iota(jnp.int32, (dec_seq, MOBA_BLOCK), 0)
    j = lax.broadcasted_iota(jnp.int32, (dec_seq, MOBA_BLOCK), 1)
    last_ref[...] = _t5_bias(MOBA_BLOCK + q - j, h, t5_ref, bounds)
    qn = lax.broadcasted_iota(jnp.int32, (dec_seq, 128), 0)
    jn = lax.broadcasted_iota(jnp.int32, (dec_seq, 128), 1)
    new_ref[...] = _t5_bias(qn - jn, h, t5_ref, bounds)


def _t5_tables(t5_table, dec_seq):
    bounds = _t5_bucket_upper_bounds(2 * MOBA_BLOCK + dec_seq)
    return pl.pallas_call(
        functools.partial(_t5_kernel, bounds=bounds, dec_seq=dec_seq),
        grid=(HEADS,),
        in_specs=[pl.BlockSpec(memory_space=pltpu.SMEM)],
        out_specs=[pl.BlockSpec((None, 2, MOBA_BLOCK, MOBA_BLOCK), lambda h: (h, 0, 0, 0)),
                   pl.BlockSpec((None, dec_seq, MOBA_BLOCK), lambda h: (h, 0, 0)),
                   pl.BlockSpec((None, dec_seq, 128), lambda h: (h, 0, 0))],
        out_shape=[jax.ShapeDtypeStruct((HEADS, 2, MOBA_BLOCK, MOBA_BLOCK), F32),
                   jax.ShapeDtypeStruct((HEADS, dec_seq, MOBA_BLOCK), F32),
                   jax.ShapeDtypeStruct((HEADS, dec_seq, 128), F32)],
        compiler_params=_params(("arbitrary",)),
        name="t5_bias",
    )(t5_table)


def _mla_proj_kernel(x_ref, g_ref, wa_ref, qan_ref, wqb_ref, kvn_ref, qng_ref, qrg_ref, krg_ref, kng_ref,
                     cos_ref, sin_ref, wkv_ref, c_ref, kpe_ref, *outs, sample):
    tm = x_ref.shape[0]
    xn = _rms(x_ref[...], g_ref[...]).astype(BF16)
    pa = _dot(xn, wa_ref[...])
    cosf, sinf = cos_ref[...], sin_ref[...]
    c = _rms(pa[:, Q_LORA:Q_LORA + KV_LORA], kvn_ref[...])
    kpe = _rope(_rms(pa[:, Q_LORA + KV_LORA:], krg_ref[...]), cosf, sinf)
    c_ref[...] = c
    kpe_ref[...] = kpe
    q = _dot(_rms(pa[:, :Q_LORA], qan_ref[...]).astype(BF16), wqb_ref[...])
    nope_w = HEADS * HEAD_DIM
    pad = jnp.zeros((tm, QK_PAD - HEAD_DIM - ROPE_DIM), F32)
    if sample:
        qabs_ref, qpe_ref = outs
        qpes = []
    else:
        qcat_ref, kcat_ref, v_ref = outs
        kv = _dot(c.astype(BF16), wkv_ref[...])
        v_ref[...] = kv[:, nope_w:].astype(BF16)
    for h in range(HEADS):
        qn = _rms(q[:, h * HEAD_DIM:(h + 1) * HEAD_DIM], qng_ref[...]) * MLA_SCALE
        qp = _rope(_rms(q[:, nope_w + h * ROPE_DIM:nope_w + (h + 1) * ROPE_DIM], qrg_ref[...]), cosf, sinf) * MLA_SCALE
        if sample:
            hi, lo = _split(qn * kng_ref[...])
            wk = wkv_ref[:, h * HEAD_DIM:(h + 1) * HEAD_DIM]
            qabs_ref[:, h * KV_LORA:(h + 1) * KV_LORA] = (_dot_nt(hi, wk) + _dot_nt(lo, wk)).astype(BF16)
            qpes.append(qp)
        else:
            kn = _rms(kv[:, h * HEAD_DIM:(h + 1) * HEAD_DIM], kng_ref[...])
            qcat_ref[:, h * QK_PAD:(h + 1) * QK_PAD] = jnp.concatenate([qn, qp, pad], axis=-1).astype(BF16)
            kcat_ref[:, h * QK_PAD:(h + 1) * QK_PAD] = jnp.concatenate([kn, kpe, pad], axis=-1).astype(BF16)
    if sample:
        qpe_ref[...] = jnp.concatenate(qpes, axis=-1).astype(BF16)


def _mla_proj(x, w, cosf, sinf, *, sample):
    t = x.shape[0]
    tm = ROW_TILE
    n_pos = cosf.shape[0] // tm
    row = lambda width: pl.BlockSpec((tm, width), lambda i: (i, 0))
    pos = pl.BlockSpec((tm, ROPE_DIM), lambda i: (i % n_pos, 0))
    consts = [w["norm_attn"], w["w_a"], w["q_a_norm"], w["w_q_b"], w["kv_a_norm"], w["qn_gain"], w["qr_gain"],
              w["kr_gain"], w["kn_gain"]]
    if sample:
        widths, dtypes = [KV_LORA, ROPE_DIM, HEADS * KV_LORA, HEADS * ROPE_DIM], [F32, F32, BF16, BF16]
    else:
        widths, dtypes = [KV_LORA, ROPE_DIM, HEADS * QK_PAD, HEADS * QK_PAD, HEADS * HEAD_DIM], [F32, F32, BF16, BF16, BF16]
    return pl.pallas_call(
        functools.partial(_mla_proj_kernel, sample=sample),
        grid=(t // tm,),
        in_specs=[row(D_MODEL)] + [_const_spec(a.shape) for a in consts] + [pos, pos, _const_spec(w["w_kv"].shape)],
        out_specs=[row(n) for n in widths],
        out_shape=[jax.ShapeDtypeStruct((t, n), d) for n, d in zip(widths, dtypes)],
        compiler_params=_params(("parallel",)),
        name="mla_proj_sample" if sample else "mla_proj_prompt",
    )(x, *consts, cosf, sinf, w["w_kv"])


def _moba_proj_kernel(x_ref, g_ref, wm_ref, qg_ref, kg_ref, q_ref, k_ref, v_ref, kb_ref, vb_ref, *mean_ref):
    xn = _rms(x_ref[...], g_ref[...]).astype(BF16)
    pm = _dot(xn, wm_ref[...])
    w = HEADS * HEAD_DIM
    for h in range(HEADS):
        sl = slice(h * HEAD_DIM, (h + 1) * HEAD_DIM)
        q_ref[:, sl] = _rms(pm[:, h * HEAD_DIM:(h + 1) * HEAD_DIM], qg_ref[...])
        k_ref[:, sl] = _rms(pm[:, w + h * HEAD_DIM:w + (h + 1) * HEAD_DIM], kg_ref[...])
    v = pm[:, 2 * w:]
    v_ref[...] = v
    vb_ref[...] = v.astype(BF16)
    k = k_ref[...]
    kb_ref[...] = k.astype(BF16)
    if mean_ref:
        mean_ref[0][...] = jnp.mean(k, axis=0, keepdims=True)


def _moba_proj(x, w, *, with_means):
    t = x.shape[0]
    tm = ROW_TILE
    width = HEADS * HEAD_DIM
    row = lambda: pl.BlockSpec((tm, width), lambda i: (i, 0))
    consts = [w["norm_attn"], w["w_m"], w["moba_q_gain"], w["moba_k_gain"]]
    out_specs = [row() for _ in range(5)]
    out_shape = [jax.ShapeDtypeStruct((t, width), d) for d in (F32, F32, F32, BF16, BF16)]
    if with_means:
        assert tm == MOBA_BLOCK
        out_specs.append(pl.BlockSpec((None, 1, width), lambda i: (i, 0, 0)))
        out_shape.append(jax.ShapeDtypeStruct((t // tm, 1, width), F32))
    return pl.pallas_call(
        _moba_proj_kernel,
        grid=(t // tm,),
        in_specs=[pl.BlockSpec((tm, D_MODEL), lambda i: (i, 0))] + [_const_spec(a.shape) for a in consts],
        out_specs=out_specs,
        out_shape=out_shape,
        compiler_params=_params(("parallel",)),
        name="moba_proj_prompt" if with_means else "moba_proj_sample",
    )(x, *consts)


def _online_softmax_step(s, v, m, l, acc):
    m_new = jnp.maximum(m, jnp.max(s, axis=-1, keepdims=True))
    a = jnp.exp(m - m_new)
    p = jnp.exp(s - m_new)
    return m_new, a * l + jnp.sum(p, axis=-1, keepdims=True), a * acc + _dot(p.astype(BF16), v)


def _mla_prompt_kernel(q_ref, k_ref, v_ref, o_ref):
    qi = pl.program_id(2)
    tq = q_ref.shape[0]
    q = q_ref[...]
    d0 = pl.multiple_of(qi * tq, tq)
    s = _dot_nt(q, k_ref[pl.ds(d0, tq), :])
    r = lax.broadcasted_iota(jnp.int32, (tq, tq), 0)
    c = lax.broadcasted_iota(jnp.int32, (tq, tq), 1)
    s = jnp.where(c <= r, s, NEG)
    m = jnp.max(s, axis=-1, keepdims=True)
    p = jnp.exp(s - m)
    l = jnp.sum(p, axis=-1, keepdims=True)
    acc = _dot(p.astype(BF16), v_ref[pl.ds(d0, tq), :])

    def body(j, carry):
        j0 = pl.multiple_of(j * tq, tq)
        return _online_softmax_step(_dot_nt(q, k_ref[pl.ds(j0, tq), :]), v_ref[pl.ds(j0, tq), :], *carry)

    m, l, acc = lax.fori_loop(0, qi, body, (m, l, acc))
    o_ref[...] = (acc / l).astype(o_ref.dtype)


def _mla_prompt(qcat, kcat, v, batch, seq):
    tq = min(MLA_TILE, seq)
    nq = seq // tq
    return pl.pallas_call(
        _mla_prompt_kernel,
        grid=(batch, HEADS, nq),
        in_specs=[pl.BlockSpec((tq, QK_PAD), lambda b, h, i: (b * nq + i, h)),
                  pl.BlockSpec((seq, QK_PAD), lambda b, h, i: (b, h)),
                  pl.BlockSpec((seq, HEAD_DIM), lambda b, h, i: (b, h))],
        out_specs=pl.BlockSpec((tq, HEAD_DIM), lambda b, h, i: (b * nq + i, h)),
        out_shape=jax.ShapeDtypeStruct((batch * seq, HEADS * HEAD_DIM), BF16),
        compiler_params=_params(("parallel", "parallel", "arbitrary")),
        name="mla_prompt_attn",
    )(qcat, kcat, v)


def _moba_prompt_kernel(t5_ref, q_ref, k_ref, v_ref, mean_ref, tiles_ref, o_ref):
    h = pl.program_id(1)
    i = pl.program_id(2)
    nb = mean_ref.shape[0]
    blk = MOBA_BLOCK
    qf = q_ref[...]
    q = (qf * MOBA_SCALE).astype(BF16)

    qh, ql = _split(qf)
    mh, ml = _split(mean_ref[...])
    gate = _dot_nt(qh, mh) + _dot_nt(qh, ml) + _dot_nt(ql, mh)
    col = lax.broadcasted_iota(jnp.int32, (blk, nb), 1)
    gate = jnp.where(col < i, gate, NEG)
    rank = jnp.zeros((blk, nb), F32)
    for jp in range(nb):
        gj = gate[:, jp:jp + 1]
        rank = rank + jnp.where(col > jp, jnp.where(gj >= gate, 1.0, 0.0), jnp.where(gj > gate, 1.0, 0.0))
    sel = jnp.where((rank < MOBA_TOPK) & (col < i), 1.0, 0.0)

    d0 = pl.multiple_of(i * blk, blk)
    r = lax.broadcasted_iota(jnp.int32, (blk, blk), 0)
    c = lax.broadcasted_iota(jnp.int32, (blk, blk), 1)
    s = jnp.where(c <= r, _dot_nt(q, k_ref[pl.ds(d0, blk), :]) + tiles_ref[0], NEG)
    m = jnp.max(s, axis=-1, keepdims=True)
    p = jnp.exp(s - m)
    l = jnp.sum(p, axis=-1, keepdims=True)
    acc = _dot(p.astype(BF16), v_ref[pl.ds(d0, blk), :])
    far_bias = t5_ref[T5_BUCKETS - 1, h]

    def body(j, carry):
        j0 = pl.multiple_of(j * blk, blk)
        picked = jnp.sum(jnp.where(col == j, sel, 0.0), axis=-1, keepdims=True)
        bias = jnp.where(j == i - 1, tiles_ref[1], far_bias)
        s = jnp.where(picked > 0.0, _dot_nt(q, k_ref[pl.ds(j0, blk), :]) + bias, NEG)
        return _online_softmax_step(s, v_ref[pl.ds(j0, blk), :], *carry)

    m, l, acc = lax.fori_loop(0, i, body, (m, l, acc))
    o_ref[...] = (acc / l).astype(o_ref.dtype)


def _moba_prompt(t5_table, qm, kb, vb, means, tiles, batch, seq):
    assert seq % MOBA_BLOCK == 0
    nb = seq // MOBA_BLOCK
    blk = MOBA_BLOCK
    return pl.pallas_call(
        _moba_prompt_kernel,
        grid=(batch, HEADS, nb),
        in_specs=[pl.BlockSpec(memory_space=pltpu.SMEM),
                  pl.BlockSpec((blk, HEAD_DIM), lambda b, h, i: (b * nb + i, h)),
                  pl.BlockSpec((seq, HEAD_DIM), lambda b, h, i: (b, h)),
                  pl.BlockSpec((seq, HEAD_DIM), lambda b, h, i: (b, h)),
                  pl.BlockSpec((None, nb, HEAD_DIM), lambda b, h, i: (b, 0, h)),
                  pl.BlockSpec((None, 2, blk, blk), lambda b, h, i: (h, 0, 0, 0))],
        out_specs=pl.BlockSpec((blk, HEAD_DIM), lambda b, h, i: (b * nb + i, h)),
        out_shape=jax.ShapeDtypeStruct((batch * seq, HEADS * HEAD_DIM), BF16),
        compiler_params=_params(("parallel", "parallel", "arbitrary")),
        name="moba_prompt_attn",
    )(t5_table, qm, kb, vb, means, tiles)


def _mla_sample_kernel(pt_ref, qabs_ref, qpe_ref, cnew_ref, kpenew_ref, wkt_ref, lat_hbm, kpe_hbm, o_ref,
                       cbuf, kbuf, cb16, s_scr, sem, *, n_pages, tile):
    s_idx = pl.program_id(0)
    n_seq = pl.num_programs(0)
    slot = s_idx % 2
    past = n_pages * PAGE_SIZE
    padded = past + PAGE_SIZE
    dec = cnew_ref.shape[0]
    rows = qabs_ref.shape[0]

    def copies(seq, sl, p):
        page = pt_ref[seq * n_pages + p]
        dst = pl.ds(pl.multiple_of(p * PAGE_SIZE, PAGE_SIZE), PAGE_SIZE)
        return (pltpu.make_async_copy(lat_hbm.at[page], cbuf.at[sl, dst], sem.at[0, sl]),
                pltpu.make_async_copy(kpe_hbm.at[page], kbuf.at[sl, dst], sem.at[1, sl]))

    def start_fetch(seq, sl):
        def body(p, _):
            for cp in copies(seq, sl, p):
                cp.start()
            return 0
        lax.fori_loop(0, n_pages, body, 0)

    def wait_fetch(seq, sl):
        def body(p, _):
            for cp in copies(seq, sl, p):
                cp.wait()
            return 0
        lax.fori_loop(0, n_pages, body, 0)

    @pl.when(s_idx == 0)
    def _():
        start_fetch(s_idx, slot)

    @pl.when(s_idx + 1 < n_seq)
    def _():
        start_fetch(s_idx + 1, 1 - slot)

    cbuf[slot, pl.ds(past, PAGE_SIZE), :] = jnp.zeros((PAGE_SIZE, KV_LORA), F32)
    kbuf[slot, pl.ds(past, PAGE_SIZE), :] = jnp.zeros((PAGE_SIZE, ROPE_DIM), F32)
    cbuf[slot, pl.ds(past, dec), :] = cnew_ref[...]
    kbuf[slot, pl.ds(past, dec), :] = kpenew_ref[...]
    wait_fetch(s_idx, slot)

    qabs = qabs_ref[...]
    qpe = qpe_ref[...]
    wkt = wkt_ref[...]
    qrow = lax.broadcasted_iota(jnp.int32, (rows, tile), 0) // HEADS
    lane = lax.broadcasted_iota(jnp.int32, (rows, tile), 1)
    n_tiles = padded // tile

    def score_tile(t, _):
        t0 = pl.multiple_of(t * tile, tile)
        c_t = cbuf[slot, pl.ds(t0, tile), :].astype(BF16)
        cb16[pl.ds(t0, tile), :] = c_t
        kn_t = _dot_nt(wkt, c_t)
        ssq = [jnp.sum(jnp.square(kn_t[h * HEAD_DIM:(h + 1) * HEAD_DIM, :]), axis=0, keepdims=True)
               for h in range(HEADS)]
        inv = lax.rsqrt(jnp.concatenate(ssq, axis=0) * (1.0 / HEAD_DIM) + EPS)
        inv = jnp.concatenate([inv] * dec, axis=0)
        s_t = _dot_nt(qabs, c_t) * inv + _dot_nt(qpe, kbuf[slot, pl.ds(t0, tile), :].astype(BF16))
        s_scr[:, pl.ds(t0, tile)] = jnp.where(t0 + lane <= past + qrow, s_t, NEG)
        return 0

    lax.fori_loop(0, n_tiles, score_tile, 0)
    s_all = s_scr[...]
    m = jnp.max(s_all, axis=-1, keepdims=True)
    p_all = jnp.exp(s_all - m)
    l = jnp.sum(p_all, axis=-1, keepdims=True)
    s_scr[...] = p_all

    def pv_tile(t, acc):
        t0 = pl.multiple_of(t * tile, tile)
        return acc + _dot(s_scr[:, pl.ds(t0, tile)].astype(BF16), cb16[pl.ds(t0, tile), :])

    acc = lax.fori_loop(0, n_tiles, pv_tile, jnp.zeros((rows, KV_LORA), F32))
    o_ref[...] = acc / l


def _largest_tile(total, unit, cap):
    n = total // unit
    best = 1
    for d in range(1, n + 1):
        if n % d == 0 and d * unit <= cap:
            best = d
    return best * unit


def _mla_sample(page_table, qabs, qpe, c_new, kpe_new, wkt, lat, kpe_cache):
    db, n_pages = page_table.shape
    dec = c_new.shape[1]
    rows = dec * HEADS
    padded = (n_pages + 1) * PAGE_SIZE
    tile = _largest_tile(padded, PAGE_SIZE, 1024)
    per_seq = lambda width: pl.BlockSpec((None, rows, width), lambda s, pt: (s, 0, 0))
    return pl.pallas_call(
        functools.partial(_mla_sample_kernel, n_pages=n_pages, tile=tile),
        grid_spec=pltpu.PrefetchScalarGridSpec(
            num_scalar_prefetch=1,
            grid=(db,),
            in_specs=[per_seq(KV_LORA), per_seq(ROPE_DIM),
                      pl.BlockSpec((None, dec, KV_LORA), lambda s, pt: (s, 0, 0)),
                      pl.BlockSpec((None, dec, ROPE_DIM), lambda s, pt: (s, 0, 0)),
                      pl.BlockSpec(wkt.shape, lambda s, pt: (0, 0)),
                      pl.BlockSpec(memory_space=pl.ANY),
                      pl.BlockSpec(memory_space=pl.ANY)],
            out_specs=per_seq(KV_LORA),
            scratch_shapes=[pltpu.VMEM((2, padded, KV_LORA), F32),
                            pltpu.VMEM((2, padded, ROPE_DIM), F32),
                            pltpu.VMEM((padded, KV_LORA), BF16),
                            pltpu.VMEM((rows, padded), F32),
                            pltpu.SemaphoreType.DMA((2, 2))]),
        out_shape=jax.ShapeDtypeStruct((db, rows, KV_LORA), F32),
        compiler_params=_params(("arbitrary",)),
        name="mla_sample_attn",
    )(page_table.reshape(-1), qabs, qpe, c_new, kpe_new, wkt, lat, kpe_cache)


def _mla_out_kernel(x_ref, wv_ref, o_ref):
    for h in range(HEADS):
        hi, lo = _split(x_ref[:, h * KV_LORA:(h + 1) * KV_LORA])
        wv = wv_ref[:, h * HEAD_DIM:(h + 1) * HEAD_DIM]
        o_ref[:, h * HEAD_DIM:(h + 1) * HEAD_DIM] = (_dot(hi, wv) + _dot(lo, wv)).astype(o_ref.dtype)


def _mla_out(x, wv):
    t = x.shape[0]
    return pl.pallas_call(
        _mla_out_kernel,
        grid=(1,),
        in_specs=[pl.BlockSpec(x.shape, lambda i: (0, 0)), pl.BlockSpec(wv.shape, lambda i: (0, 0))],
        out_specs=pl.BlockSpec((t, HEADS * HEAD_DIM), lambda i: (0, 0)),
        out_shape=jax.ShapeDtypeStruct((t, HEADS * HEAD_DIM), BF16),
        compiler_params=_params(("arbitrary",)),
        name="mla_sample_out",
    )(x, wv)


def _moba_select_kernel(pt_ref, q_ref, k_hbm, idx_ref, kbuf, sums, sem, *, n_pages, chunk_pages):
    g = pl.program_id(0)
    n_steps = pl.num_programs(0)
    n_chunks = n_pages // chunk_pages
    chunk = g % n_chunks
    slot = g % 2
    n_full = n_pages // PAGES_PER_BLOCK
    blocks_per_chunk = chunk_pages // PAGES_PER_BLOCK
    dec = q_ref.shape[0]

    def copy(step, sl, p):
        page = pt_ref[step * chunk_pages + p]
        return pltpu.make_async_copy(k_hbm.at[page], kbuf.at[sl, p], sem.at[sl])

    def start_fetch(step, sl):
        for p in range(chunk_pages):
            copy(step, sl, p).start()

    @pl.when(g == 0)
    def _():
        start_fetch(g, slot)

    @pl.when(g + 1 < n_steps)
    def _():
        start_fetch(g + 1, 1 - slot)

    for p in range(chunk_pages):
        copy(g, slot, p).wait()

    for b in range(blocks_per_chunk):
        x = kbuf[slot, b * PAGES_PER_BLOCK:(b + 1) * PAGES_PER_BLOCK]
        n = chunk * blocks_per_chunk + b
        sums[pl.ds(pl.multiple_of(n * HEADS, HEADS), HEADS), :] = jnp.sum(x, axis=(0, 1))

    @pl.when(chunk == n_chunks - 1)
    def _():
        means = sums[...] * (1.0 / MOBA_BLOCK)
        rows = n_full * HEADS
        row_blk = lax.broadcasted_iota(jnp.int32, (rows, 1), 0) // HEADS
        lane = lax.broadcasted_iota(jnp.int32, (HEADS, 128), 1)
        out = jnp.zeros((HEADS, 128), F32)
        for q in range(dec):
            gate = jnp.sum(means * jnp.concatenate([q_ref[q]] * n_full, axis=0), axis=-1, keepdims=True)
            rank = jnp.zeros((rows, 1), F32)
            for n in range(n_full):
                gn = jnp.concatenate([gate[n * HEADS:(n + 1) * HEADS]] * n_full, axis=0)
                rank = rank + jnp.where(row_blk > n, jnp.where(gn >= gate, 1.0, 0.0), jnp.where(gn > gate, 1.0, 0.0))
            for k in range(MOBA_TOPK):
                pick = jnp.zeros((HEADS, 1), F32)
                for n in range(1, n_full):
                    pick = pick + jnp.where(rank[n * HEADS:(n + 1) * HEADS] == k, float(n), 0.0)
                out = jnp.where(lane == q * MOBA_TOPK + k, pick, out)
        idx_ref[...] = out.astype(jnp.int32)


def _moba_select(page_table, q4, k4):
    db, n_pages = page_table.shape
    dec = q4.shape[1]
    chunk_pages = 8 if n_pages % 8 == 0 else PAGES_PER_BLOCK
    n_chunks = n_pages // chunk_pages
    n_full = n_pages // PAGES_PER_BLOCK
    return pl.pallas_call(
        functools.partial(_moba_select_kernel, n_pages=n_pages, chunk_pages=chunk_pages),
        grid_spec=pltpu.PrefetchScalarGridSpec(
            num_scalar_prefetch=1,
            grid=(db * n_chunks,),
            in_specs=[pl.BlockSpec((None, dec, HEADS, HEAD_DIM), lambda g, pt: (g // n_chunks, 0, 0, 0)),
                      pl.BlockSpec(memory_space=pl.ANY)],
            out_specs=pl.BlockSpec((None, HEADS, 128), lambda g, pt: (g // n_chunks, 0, 0)),
            scratch_shapes=[pltpu.VMEM((2, chunk_pages, PAGE_SIZE, HEADS, HEAD_DIM), F32),
                            pltpu.VMEM((n_full * HEADS, HEAD_DIM), F32),
                            pltpu.SemaphoreType.DMA((2,))]),
        out_shape=jax.ShapeDtypeStruct((db, HEADS, 128), jnp.int32),
        compiler_params=_params(("arbitrary",)),
        name="moba_sample_select",
    )(page_table.reshape(-1), q4, k4)


def _moba_sample_kernel(pt_ref, idx_ref, t5_ref, q_ref, kn_ref, vn_ref, last_ref, new_ref, k_hbm, v_hbm, o_ref,
                        kbuf, vbuf, sem, *, n_pages, dec):
    s_idx = pl.program_id(0)
    h = pl.program_id(1)
    g = s_idx * HEADS + h
    n_steps = pl.num_programs(0) * HEADS
    slot = g % 2
    n_full = n_pages // PAGES_PER_BLOCK
    n_sel = MOBA_TOPK * dec

    def copies(step, sl, q, k, j):
        seq = step // HEADS
        head = step % HEADS
        blk = idx_ref[step * n_sel + q * MOBA_TOPK + k]
        page = pt_ref[seq * n_pages + blk * PAGES_PER_BLOCK + j]
        dst = pl.ds((k * PAGES_PER_BLOCK + j) * PAGE_SIZE, PAGE_SIZE)
        return (pltpu.make_async_copy(k_hbm.at[page, :, head, :], kbuf.at[sl, q, dst], sem.at[0, sl]),
                pltpu.make_async_copy(v_hbm.at[page, :, head, :], vbuf.at[sl, q, dst], sem.at[1, sl]))

    def for_all(step, sl, fn):
        for q in range(dec):
            for k in range(MOBA_TOPK):
                for j in range(PAGES_PER_BLOCK):
                    for cp in copies(step, sl, q, k, j):
                        fn(cp)

    @pl.when(g == 0)
    def _():
        for_all(g, slot, lambda cp: cp.start())

    @pl.when(g + 1 < n_steps)
    def _():
        for_all(g + 1, 1 - slot, lambda cp: cp.start())

    for_all(g, slot, lambda cp: cp.wait())

    q8 = (q_ref[...] * MOBA_SCALE).astype(BF16)
    far_bias = t5_ref[T5_BUCKETS - 1, h]
    kn = kn_ref[...].astype(BF16)
    vn = vn_ref[...].astype(BF16)
    jn = lax.broadcasted_iota(jnp.int32, (1, 8), 1)
    outs = []
    for q in range(dec):
        kq = kbuf[slot, q].astype(BF16)
        vq = vbuf[slot, q].astype(BF16)
        s = _dot_nt(q8, kq)[q:q + 1, :]
        bias = [jnp.where(idx_ref[g * n_sel + q * MOBA_TOPK + k] == n_full - 1, last_ref[q:q + 1, :], far_bias)
                for k in range(MOBA_TOPK)]
        s = s + jnp.concatenate(bias, axis=-1)
        sn = _dot_nt(q8, kn)[q:q + 1, :] + new_ref[q:q + 1, :8]
        sn = jnp.where(jn <= q, sn, NEG)
        m = jnp.maximum(jnp.max(s, axis=-1, keepdims=True), jnp.max(sn, axis=-1, keepdims=True))
        p = jnp.exp(s - m)
        pn = jnp.exp(sn - m)
        l = jnp.sum(p, axis=-1, keepdims=True) + jnp.sum(pn, axis=-1, keepdims=True)
        o = _dot(jnp.broadcast_to(p, (8, p.shape[1])).astype(BF16), vq) \
            + _dot(jnp.broadcast_to(pn, (8, 8)).astype(BF16), vn)
        outs.append(o[0:1, :] / l)
    o_ref[...] = jnp.concatenate(outs, axis=0)


def _moba_sample(page_table, idx, t5_table, q8, kn8, vn8, bias_last, bias_new, k4, v4, dec):
    db, n_pages = page_table.shape
    per = lambda: pl.BlockSpec((None, None, 8, HEAD_DIM), lambda s, h, pt, ix: (s, h, 0, 0))
    return pl.pallas_call(
        functools.partial(_moba_sample_kernel, n_pages=n_pages, dec=dec),
        grid_spec=pltpu.PrefetchScalarGridSpec(
            num_scalar_prefetch=2,
            grid=(db, HEADS),
            in_specs=[pl.BlockSpec(memory_space=pltpu.SMEM), per(), per(), per(),
                      pl.BlockSpec((None, dec, MOBA_BLOCK), lambda s, h, pt, ix: (h, 0, 0)),
                      pl.BlockSpec((None, dec, 128), lambda s, h, pt, ix: (h, 0, 0)),
                      pl.BlockSpec(memory_space=pl.ANY),
                      pl.BlockSpec(memory_space=pl.ANY)],
            out_specs=pl.BlockSpec((None, dec, HEAD_DIM), lambda s, h, pt, ix: (s, 0, h)),
            scratch_shapes=[pltpu.VMEM((2, dec, MOBA_TOPK * MOBA_BLOCK, HEAD_DIM), F32),
                            pltpu.VMEM((2, dec, MOBA_TOPK * MOBA_BLOCK, HEAD_DIM), F32),
                            pltpu.SemaphoreType.DMA((2, 2))]),
        out_shape=jax.ShapeDtypeStruct((db, dec, HEADS * HEAD_DIM), F32),
        compiler_params=_params(("arbitrary", "arbitrary")),
        name="moba_sample_attn",
    )(page_table.reshape(-1), idx, t5_table, q8, kn8, vn8, bias_last, bias_new, k4, v4)


def _merge_ffn_kernel(x_ref, oa_ref, ob_ref, ga_ref, wg_ref, wo_ref, gf_ref, wgu_ref, wd_ref, y_ref):
    x = x_ref[...]
    xn = _rms(x, ga_ref[...]).astype(BF16)
    g = _dot(xn, wg_ref[...])
    mix = jax.nn.sigmoid(g[:, :D_MODEL]) * oa_ref[...].astype(F32) + jax.nn.sigmoid(g[:, D_MODEL:]) * ob_ref[...].astype(F32)
    hres = x + _dot(mix.astype(BF16), wo_ref[...])
    gu = _dot(_rms(hres, gf_ref[...]).astype(BF16), wgu_ref[...])
    d_ff = wd_ref.shape[0]
    act = jax.nn.silu(gu[:, :d_ff]) * gu[:, d_ff:]
    y_ref[...] = hres + _dot(act.astype(BF16), wd_ref[...])


def _merge_ffn(x, o_a, o_b, w):
    t = x.shape[0]
    tm = ROW_TILE
    row = lambda: pl.BlockSpec((tm, D_MODEL), lambda i: (i, 0))
    consts = [w["norm_attn"], w["w_g"], w["w_o"], w["norm_ffn"], w["w_gate_up"], w["w_down"]]
    return pl.pallas_call(
        _merge_ffn_kernel,
        grid=(t // tm,),
        in_specs=[row(), row(), row()] + [_const_spec(a.shape) for a in consts],
        out_specs=row(),
        out_shape=jax.ShapeDtypeStruct((t, D_MODEL), F32),
        compiler_params=_params(("parallel",)),
        name="merge_ffn",
    )(x, o_a, o_b, *consts)


def _rope_tables(pos):
    half = ROPE_DIM // 2
    freqs = ROPE_THETA ** (-jnp.arange(half, dtype=F32) / half)
    ang = pos.astype(F32)[:, None] * freqs
    cos, sin = jnp.cos(ang), jnp.sin(ang)
    return jnp.concatenate([cos, cos], axis=-1), jnp.concatenate([-sin, sin], axis=-1)


def _layer_weights(l, norm_attn, w_in, q_a_norm, w_q_b, kv_a_norm, w_kv_b, mla_qn_gain, mla_qr_gain, mla_kn_gain,
                   mla_kr_gain, moba_q_gain, moba_k_gain, w_o, norm_ffn, w_gate_up, w_down):
    width = HEADS * HEAD_DIM
    a_end = Q_LORA + KV_LORA + ROPE_DIM
    row = lambda a: a[l].reshape(1, -1)
    win = w_in[l].astype(BF16)
    wqb = w_q_b[l].reshape(Q_LORA, HEADS, HEAD_DIM + ROPE_DIM)
    wkv = w_kv_b[l].reshape(KV_LORA, HEADS, 2 * HEAD_DIM)
    return {
        "norm_attn": row(norm_attn), "q_a_norm": row(q_a_norm), "kv_a_norm": row(kv_a_norm),
        "qn_gain": row(mla_qn_gain), "qr_gain": row(mla_qr_gain), "kn_gain": row(mla_kn_gain),
        "kr_gain": row(mla_kr_gain), "moba_q_gain": row(moba_q_gain), "moba_k_gain": row(moba_k_gain),
        "norm_ffn": row(norm_ffn),
        "w_a": win[:, :a_end], "w_m": win[:, a_end:a_end + 3 * width], "w_g": win[:, a_end + 3 * width:],
        "w_q_b": jnp.concatenate([wqb[:, :, :HEAD_DIM].reshape(Q_LORA, -1),
                                  wqb[:, :, HEAD_DIM:].reshape(Q_LORA, -1)], axis=-1).astype(BF16),
        "w_kv": jnp.concatenate([wkv[:, :, :HEAD_DIM].reshape(KV_LORA, -1),
                                 wkv[:, :, HEAD_DIM:].reshape(KV_LORA, -1)], axis=-1).astype(BF16),
        "w_o": w_o[l].astype(BF16), "w_gate_up": w_gate_up[l].astype(BF16), "w_down": w_down[l].astype(BF16),
    }


def kernel(x_prompt, x_sample, cache_mla_latent, cache_mla_kpe, cache_moba_k, cache_moba_v, page_table, norm_attn, w_in, q_a_norm, w_q_b, kv_a_norm, w_kv_b, mla_qn_gain, mla_qr_gain, mla_kn_gain, mla_kr_gain, moba_q_gain, moba_k_gain, t5_table, w_o, norm_ffn, w_gate_up, w_down):
    batch, seq, _ = x_prompt.shape
    db, dec, _ = x_sample.shape
    depth = w_in.shape[0]
    n_pool = cache_mla_latent.shape[1]
    n_pages = page_table.shape[1]
    past = n_pages * PAGE_SIZE
    width = HEADS * HEAD_DIM
    assert seq % ROW_TILE == 0 and (db * dec) % ROW_TILE == 0 and ROW_TILE % dec == 0
    assert past % MOBA_BLOCK == 0 and past // MOBA_BLOCK >= MOBA_TOPK and dec <= 8

    cos_p, sin_p = _rope_tables(jnp.arange(seq))
    cos_s, sin_s = _rope_tables(past + jnp.arange(ROW_TILE) % dec)
    tiles, bias_last, bias_new = _t5_tables(t5_table, dec)

    yp = x_prompt.reshape(batch * seq, D_MODEL)
    ys = x_sample.reshape(db * dec, D_MODEL)
    outs = [[] for _ in range(8)]
    for l in range(depth):
        w = _layer_weights(l, norm_attn, w_in, q_a_norm, w_q_b, kv_a_norm, w_kv_b, mla_qn_gain, mla_qr_gain,
                           mla_kn_gain, mla_kr_gain, moba_q_gain, moba_k_gain, w_o, norm_ffn, w_gate_up, w_down)
        c, kpe, qcat, kcat, v = _mla_proj(yp, w, cos_p, sin_p, sample=False)
        qm, km, vm, kb, vb, means = _moba_proj(yp, w, with_means=True)
        o_a = _mla_prompt(qcat, kcat, v, batch, seq)
        o_b = _moba_prompt(t5_table, qm, kb, vb, means.reshape(batch, seq // MOBA_BLOCK, width), tiles, batch, seq)
        outs[0].append(c.reshape(batch, seq, KV_LORA))
        outs[1].append(kpe.reshape(batch, seq, ROPE_DIM))
        outs[2].append(km.reshape(batch, seq, HEADS, HEAD_DIM))
        outs[3].append(vm.reshape(batch, seq, HEADS, HEAD_DIM))
        yp = _merge_ffn(yp, o_a, o_b, w)
        c, kpe, qabs, qpe = _mla_proj(ys, w, cos_s, sin_s, sample=True)
        qm, km, vm, kb, vb = _moba_proj(ys, w, with_means=False)
        lat = _mla_sample(page_table, qabs.reshape(db, dec * HEADS, KV_LORA), qpe.reshape(db, dec * HEADS, ROPE_DIM),
                          c.reshape(db, dec, KV_LORA), kpe.reshape(db, dec, ROPE_DIM),
                          w["w_kv"][:, :width].T, cache_mla_latent[l], cache_mla_kpe[l])
        o_a = _mla_out(lat.reshape(db * dec, HEADS * KV_LORA), w["w_kv"][:, width:])
        k4 = cache_moba_k[l]
        v4 = cache_moba_v[l]
        idx = _moba_select(page_table, qm.reshape(db, dec, HEADS, HEAD_DIM), k4)
        idx = idx[:, :, :MOBA_TOPK * dec].reshape(-1)
        to_heads = lambda a: jnp.pad(jnp.transpose(a.reshape(db, dec, HEADS, HEAD_DIM), (0, 2, 1, 3)),
                                     ((0, 0), (0, 0), (0, 8 - dec), (0, 0)))
        o_b = _moba_sample(page_table, idx, t5_table, to_heads(qm), to_heads(km), to_heads(vm), bias_last, bias_new,
                           k4, v4, dec)
        outs[4].append(c.reshape(db, dec, KV_LORA))
        outs[5].append(kpe.reshape(db, dec, ROPE_DIM))
        outs[6].append(km.reshape(db, dec, HEADS, HEAD_DIM))
        outs[7].append(vm.reshape(db, dec, HEADS, HEAD_DIM))
        ys = _merge_ffn(ys, o_a, o_b.reshape(db * dec, width), w)
    return (yp.reshape(batch, seq, D_MODEL), ys.reshape(db, dec, D_MODEL)) + tuple(jnp.stack(o, axis=0) for o in outs)
```

```python
import functools
import math

import numpy as np
import jax
import jax.numpy as jnp
from jax import lax
from jax.experimental import pallas as pl
from jax.experimental.pallas import tpu as pltpu

F32 = jnp.float32
BF16 = jnp.bfloat16

D_MODEL = 1024
HEADS = 8
HEAD_DIM = 128
ROPE_DIM = 64
Q_LORA = 256
KV_LORA = 256
QK_PAD = 256
ROPE_THETA = 10000.0
MLA_SCALE = (HEAD_DIM + ROPE_DIM) ** -0.5
MOBA_BLOCK = 256
MOBA_TOPK = 3
MOBA_SCALE = HEAD_DIM ** -0.5
PAGE_SIZE = 128
PAGES_PER_BLOCK = MOBA_BLOCK // PAGE_SIZE
T5_BUCKETS = 32
T5_MAX_DIST = 128
EPS = 1e-6
NEG = -1e30
LOG2E = math.log2(math.e)
ROW_TILE = 256
MLA_TILE = 512
VMEM_LIMIT = 56 * 1024 * 1024

_NT = (((1,), (1,)), ((), ()))


def _dot(a, b):
    return jnp.dot(a, b, preferred_element_type=F32)


def _dot_nt(a, b):
    return lax.dot_general(a, b, _NT, preferred_element_type=F32)


def _split(x):
    hi = x.astype(BF16)
    lo = (x - hi.astype(F32)).astype(BF16)
    return hi, lo


def _rms(x, g):
    ms = jnp.mean(x * x, axis=-1, keepdims=True)
    return x * lax.rsqrt(ms + EPS) * g


def _rope(x, cosf, sinf):
    half = ROPE_DIM // 2
    swapped = jnp.concatenate([x[:, half:], x[:, :half]], axis=-1)
    return x * cosf + swapped * sinf


def _const_spec(shape):
    n = len(shape)
    return pl.BlockSpec(shape, lambda *_: (0,) * n, pipeline_mode=pl.Buffered(1))


def _params(sem):
    return pltpu.CompilerParams(dimension_semantics=sem, vmem_limit_bytes=VMEM_LIMIT)


def _t5_bucket_upper_bounds(max_dist):
    d = np.arange(max_dist + 1)
    max_exact = T5_BUCKETS // 2
    dd = np.maximum(d, max_exact).astype(np.float32)
    large = max_exact + (np.log(dd / np.float32(max_exact)) / np.float32(math.log(T5_MAX_DIST / max_exact))
                         * np.float32(T5_BUCKETS - max_exact)).astype(np.int32)
    bucket = np.where(d < max_exact, d, np.minimum(large, T5_BUCKETS - 1))
    assert bucket[-1] == T5_BUCKETS - 1
    return [int(d[bucket == b].max()) if np.any(bucket == b) else None for b in range(T5_BUCKETS)]


def _t5_bias(dist, h, t5_ref, bounds):
    res = jnp.full(dist.shape, t5_ref[T5_BUCKETS - 1, h], F32)
    for b in range(T5_BUCKETS - 2, -1, -1):
        if bounds[b] is not None:
            res = jnp.where(dist <= bounds[b], t5_ref[b, h], res)
    return res


def _t5_kernel(t5_ref, tiles_ref, last_ref, new_ref, *, bounds, dec_seq):
    h = pl.program_id(0)
    far = t5_ref[T5_BUCKETS - 1, h]
    rel = lambda dist: (_t5_bias(dist, h, t5_ref, bounds) - far) * LOG2E
    r = lax.broadcasted_iota(jnp.int32, (MOBA_BLOCK, MOBA_BLOCK), 0)
    c = lax.broadcasted_iota(jnp.int32, (MOBA_BLOCK, MOBA_BLOCK), 1)
    tiles_ref[0] = jnp.where(c <= r, rel(r - c), NEG)
    tiles_ref[1] = rel(MOBA_BLOCK + r - c)
    q = lax.broadcasted_iota(jnp.int32, (dec_seq, MOBA_BLOCK), 0)
    j = lax.broadcasted_iota(jnp.int32, (dec_seq, MOBA_BLOCK), 1)
    last_ref[...] = rel(MOBA_BLOCK + q - j)
    qn = lax.broadcasted_iota(jnp.int32, (dec_seq, 128), 0)
    jn = lax.broadcasted_iota(jnp.int32, (dec_seq, 128), 1)
    new_ref[...] = jnp.where(jn <= qn, rel(qn - jn), NEG)


def _t5_tables(t5_table, dec_seq):
    bounds = _t5_bucket_upper_bounds(2 * MOBA_BLOCK + dec_seq)
    return pl.pallas_call(
        functools.partial(_t5_kernel, bounds=bounds, dec_seq=dec_seq),
        grid=(HEADS,),
        in_specs=[pl.BlockSpec(memory_space=pltpu.SMEM)],
        out_specs=[pl.BlockSpec((None, 2, MOBA_BLOCK, MOBA_BLOCK), lambda h: (h, 0, 0, 0)),
                   pl.BlockSpec((None, dec_seq, MOBA_BLOCK), lambda h: (h, 0, 0)),
                   pl.BlockSpec((None, dec_seq, 128), lambda h: (h, 0, 0))],
        out_shape=[jax.ShapeDtypeStruct((HEADS, 2, MOBA_BLOCK, MOBA_BLOCK), F32),
                   jax.ShapeDtypeStruct((HEADS, dec_seq, MOBA_BLOCK), F32),
                   jax.ShapeDtypeStruct((HEADS, dec_seq, 128), F32)],
        compiler_params=_params(("arbitrary",)),
        name="t5_bias",
    )(t5_table)


def _mla_proj_kernel(x_ref, g_ref, wa_ref, qan_ref, wqb_ref, kvn_ref, qng_ref, qrg_ref, krg_ref, kng_ref,
                     cos_ref, sin_ref, wkv_ref, c_ref, kpe_ref, *outs, sample):
    tm = x_ref.shape[0]
    xn = _rms(x_ref[...], g_ref[...]).astype(BF16)
    pa = _dot(xn, wa_ref[...])
    cosf, sinf = cos_ref[...], sin_ref[...]
    c = _rms(pa[:, Q_LORA:Q_LORA + KV_LORA], kvn_ref[...])
    kpe = _rope(_rms(pa[:, Q_LORA + KV_LORA:], krg_ref[...]), cosf, sinf)
    c_ref[...] = c
    kpe_ref[...] = kpe
    q = _dot(_rms(pa[:, :Q_LORA], qan_ref[...]).astype(BF16), wqb_ref[...])
    nope_w = HEADS * HEAD_DIM
    pad = jnp.zeros((tm, QK_PAD - HEAD_DIM - ROPE_DIM), F32)
    if sample:
        qabs_ref, qpe_ref = outs
        qpes = []
    else:
        qcat_ref, kcat_ref, v_ref = outs
        kv = _dot(c.astype(BF16), wkv_ref[...])
        v_ref[...] = kv[:, nope_w:].astype(BF16)
    for h in range(HEADS):
        qn = _rms(q[:, h * HEAD_DIM:(h + 1) * HEAD_DIM], qng_ref[...]) * (MLA_SCALE * LOG2E)
        qp = _rope(_rms(q[:, nope_w + h * ROPE_DIM:nope_w + (h + 1) * ROPE_DIM], qrg_ref[...]), cosf, sinf) * (MLA_SCALE * LOG2E)
        if sample:
            hi, lo = _split(qn * kng_ref[...])
            wk = wkv_ref[:, h * HEAD_DIM:(h + 1) * HEAD_DIM]
            qabs_ref[:, h * KV_LORA:(h + 1) * KV_LORA] = (_dot_nt(hi, wk) + _dot_nt(lo, wk)).astype(BF16)
            qpes.append(qp)
        else:
            kn = _rms(kv[:, h * HEAD_DIM:(h + 1) * HEAD_DIM], kng_ref[...])
            qcat_ref[:, h * QK_PAD:(h + 1) * QK_PAD] = jnp.concatenate([qn, qp, pad], axis=-1).astype(BF16)
            kcat_ref[:, h * QK_PAD:(h + 1) * QK_PAD] = jnp.concatenate([kn, kpe, pad], axis=-1).astype(BF16)
    if sample:
        qpe_ref[...] = jnp.concatenate(qpes, axis=-1).astype(BF16)


def _mla_proj(x, w, cosf, sinf, *, sample):
    t = x.shape[0]
    tm = ROW_TILE
    n_pos = cosf.shape[0] // tm
    row = lambda width: pl.BlockSpec((tm, width), lambda i: (i, 0))
    pos = pl.BlockSpec((tm, ROPE_DIM), lambda i: (i % n_pos, 0))
    consts = [w["norm_attn"], w["w_a"], w["q_a_norm"], w["w_q_b"], w["kv_a_norm"], w["qn_gain"], w["qr_gain"],
              w["kr_gain"], w["kn_gain"]]
    if sample:
        widths, dtypes = [KV_LORA, ROPE_DIM, HEADS * KV_LORA, HEADS * ROPE_DIM], [F32, F32, BF16, BF16]
    else:
        widths, dtypes = [KV_LORA, ROPE_DIM, HEADS * QK_PAD, HEADS * QK_PAD, HEADS * HEAD_DIM], [F32, F32, BF16, BF16, BF16]
    return pl.pallas_call(
        functools.partial(_mla_proj_kernel, sample=sample),
        grid=(t // tm,),
        in_specs=[row(D_MODEL)] + [_const_spec(a.shape) for a in consts] + [pos, pos, _const_spec(w["w_kv"].shape)],
        out_specs=[row(n) for n in widths],
        out_shape=[jax.ShapeDtypeStruct((t, n), d) for n, d in zip(widths, dtypes)],
        compiler_params=_params(("parallel",)),
        name="mla_proj_sample" if sample else "mla_proj_prompt",
    )(x, *consts, cosf, sinf, w["w_kv"])


def _moba_proj_kernel(x_ref, g_ref, wm_ref, qg_ref, kg_ref, q_ref, k_ref, v_ref, *prompt_refs, blocks_per_seq):
    tm = x_ref.shape[0]
    xn = _rms(x_ref[...], g_ref[...]).astype(BF16)
    pm = _dot(xn, wm_ref[...])
    w = HEADS * HEAD_DIM
    v = pm[:, 2 * w:]
    v_ref[...] = v
    if prompt_refs:
        kext_ref, vb_ref, mean_ref = prompt_refs
        vb_ref[...] = v.astype(BF16)
        lane = lax.broadcasted_iota(jnp.int32, (tm, QK_PAD - HEAD_DIM), 1)
        onehot = jnp.where(lane == pl.program_id(0) % blocks_per_seq, 1.0, 0.0).astype(BF16)
    for h in range(HEADS):
        sl = slice(h * HEAD_DIM, (h + 1) * HEAD_DIM)
        q_ref[:, sl] = _rms(pm[:, h * HEAD_DIM:(h + 1) * HEAD_DIM], qg_ref[...])
        k = _rms(pm[:, w + h * HEAD_DIM:w + (h + 1) * HEAD_DIM], kg_ref[...])
        k_ref[:, sl] = k
        if prompt_refs:
            kext_ref[:, h * QK_PAD:(h + 1) * QK_PAD] = jnp.concatenate([k.astype(BF16), onehot], axis=-1)
            mean_ref[:, sl] = jnp.mean(k, axis=0, keepdims=True)


def _moba_proj(x, w, *, blocks_per_seq=None):
    t = x.shape[0]
    tm = ROW_TILE
    width = HEADS * HEAD_DIM
    prompt = blocks_per_seq is not None
    row = lambda n=width: pl.BlockSpec((tm, n), lambda i: (i, 0))
    consts = [w["norm_attn"], w["w_m"], w["moba_q_gain"], w["moba_k_gain"]]
    out_specs = [row() for _ in range(3)]
    out_shape = [jax.ShapeDtypeStruct((t, width), F32) for _ in range(3)]
    if prompt:
        assert tm == MOBA_BLOCK and blocks_per_seq <= QK_PAD - HEAD_DIM
        out_specs += [row(HEADS * QK_PAD), row(), pl.BlockSpec((None, 1, width), lambda i: (i, 0, 0))]
        out_shape += [jax.ShapeDtypeStruct((t, HEADS * QK_PAD), BF16), jax.ShapeDtypeStruct((t, width), BF16),
                      jax.ShapeDtypeStruct((t // tm, 1, width), F32)]
    return pl.pallas_call(
        functools.partial(_moba_proj_kernel, blocks_per_seq=blocks_per_seq),
        grid=(t // tm,),
        in_specs=[pl.BlockSpec((tm, D_MODEL), lambda i: (i, 0))] + [_const_spec(a.shape) for a in consts],
        out_specs=out_specs,
        out_shape=out_shape,
        compiler_params=_params(("parallel",)),
        name="moba_proj_prompt" if prompt else "moba_proj_sample",
    )(x, *consts)


def _online_softmax_step(s, v, m, l, acc):
    m_new = jnp.maximum(m, jnp.max(s, axis=-1, keepdims=True))
    a = jnp.exp2(m - m_new)
    p = jnp.exp2(s - m_new)
    return m_new, a * l + jnp.sum(p, axis=-1, keepdims=True), a * acc + _dot(p.astype(BF16), v)


def _mla_prompt_kernel(q_ref, k_ref, v_ref, o_ref):
    qi = pl.program_id(2)
    tq = q_ref.shape[0]
    q = q_ref[...]
    d0 = pl.multiple_of(qi * tq, tq)
    s = _dot_nt(q, k_ref[pl.ds(d0, tq), :])
    r = lax.broadcasted_iota(jnp.int32, (tq, tq), 0)
    c = lax.broadcasted_iota(jnp.int32, (tq, tq), 1)
    s = jnp.where(c <= r, s, NEG)
    m = jnp.max(s, axis=-1, keepdims=True)
    p = jnp.exp2(s - m)
    l = jnp.sum(p, axis=-1, keepdims=True)
    acc = _dot(p.astype(BF16), v_ref[pl.ds(d0, tq), :])

    def body(j, carry):
        j0 = pl.multiple_of(j * tq, tq)
        return _online_softmax_step(_dot_nt(q, k_ref[pl.ds(j0, tq), :]), v_ref[pl.ds(j0, tq), :], *carry)

    m, l, acc = lax.fori_loop(0, qi, body, (m, l, acc))
    o_ref[...] = (acc / l).astype(o_ref.dtype)


def _mla_prompt(qcat, kcat, v, batch, seq):
    tq = min(MLA_TILE, seq)
    nq = seq // tq
    return pl.pallas_call(
        _mla_prompt_kernel,
        grid=(batch, HEADS, nq),
        in_specs=[pl.BlockSpec((tq, QK_PAD), lambda b, h, i: (b * nq + i, h)),
                  pl.BlockSpec((seq, QK_PAD), lambda b, h, i: (b, h)),
                  pl.BlockSpec((seq, HEAD_DIM), lambda b, h, i: (b, h))],
        out_specs=pl.BlockSpec((tq, HEAD_DIM), lambda b, h, i: (b * nq + i, h)),
        out_shape=jax.ShapeDtypeStruct((batch * seq, HEADS * HEAD_DIM), BF16),
        compiler_params=_params(("parallel", "parallel", "arbitrary")),
        name="mla_prompt_attn",
    )(qcat, kcat, v)


def _moba_prompt_kernel(q_ref, k_ref, v_ref, mean_ref, tiles_ref, o_ref, *, group):
    i = pl.program_id(2)
    nb = mean_ref.shape[0]
    blk = MOBA_BLOCK
    pad = QK_PAD - HEAD_DIM
    qf = q_ref[...]

    qh, ql = _split(qf)
    mh, ml = _split(mean_ref[...])
    gate = _dot_nt(mh, qh) + _dot_nt(ml, qh) + _dot_nt(mh, ql)
    row = lax.broadcasted_iota(jnp.int32, (nb, blk), 0)
    gate = jnp.where(row < i, gate, NEG)
    rank = jnp.zeros((nb, blk), F32)
    for jp in range(nb):
        gj = gate[jp:jp + 1, :]
        rank = rank + jnp.where(row > jp, jnp.where(gj >= gate, 1.0, 0.0), jnp.where(gj > gate, 1.0, 0.0))
    sel = (rank < MOBA_TOPK) & (row < i)
    near = jnp.where((row == i) | ((row == i - 1) & sel), 0.0, NEG)
    far = jnp.where((row < i - 1) & sel, 0.0, NEG)
    fill = jnp.zeros((pad - nb, blk), F32)
    near_t = jnp.concatenate([near, fill], axis=0).T
    far_t = jnp.concatenate([far, fill], axis=0).T
    qs = (qf * (MOBA_SCALE * LOG2E)).astype(BF16)
    q_near = jnp.concatenate([qs, near_t.astype(BF16)], axis=-1)
    q_far = jnp.concatenate([qs, far_t.astype(BF16)], axis=-1)

    d0 = pl.multiple_of(i * blk, blk)
    p0 = pl.multiple_of(jnp.maximum(i - 1, 0) * blk, blk)
    s_own = _dot_nt(q_near, k_ref[pl.ds(d0, blk), :]) + tiles_ref[0]
    s_prev = _dot_nt(q_near, k_ref[pl.ds(p0, blk), :]) + (tiles_ref[1] + jnp.where(i == 0, NEG, 0.0))
    m = jnp.maximum(jnp.max(s_own, axis=-1, keepdims=True), jnp.max(s_prev, axis=-1, keepdims=True))
    p_own = jnp.exp2(s_own - m)
    p_prev = jnp.exp2(s_prev - m)
    l = jnp.sum(p_own, axis=-1, keepdims=True) + jnp.sum(p_prev, axis=-1, keepdims=True)
    acc = _dot(p_own.astype(BF16), v_ref[pl.ds(d0, blk), :]) + _dot(p_prev.astype(BF16), v_ref[pl.ds(p0, blk), :])

    span = group * blk

    def body(g, carry):
        g0 = pl.multiple_of(g * span, span)
        return _online_softmax_step(_dot_nt(q_far, k_ref[pl.ds(g0, span), :]), v_ref[pl.ds(g0, span), :], *carry)

    n_groups = (jnp.maximum(i - 1, 0) + group - 1) // group
    m, l, acc = lax.fori_loop(0, n_groups, body, (m, l, acc))
    o_ref[...] = (acc / l).astype(o_ref.dtype)


def _moba_prompt(qm, kext, vb, means, tiles, batch, seq):
    assert seq % MOBA_BLOCK == 0
    nb = seq // MOBA_BLOCK
    blk = MOBA_BLOCK
    group = math.gcd(nb, 4)
    return pl.pallas_call(
        functools.partial(_moba_prompt_kernel, group=group),
        grid=(batch, HEADS, nb),
        in_specs=[pl.BlockSpec((blk, HEAD_DIM), lambda b, h, i: (b * nb + i, h)),
                  pl.BlockSpec((seq, QK_PAD), lambda b, h, i: (b, h)),
                  pl.BlockSpec((seq, HEAD_DIM), lambda b, h, i: (b, h)),
                  pl.BlockSpec((None, nb, HEAD_DIM), lambda b, h, i: (b, 0, h)),
                  pl.BlockSpec((None, 2, blk, blk), lambda b, h, i: (h, 0, 0, 0))],
        out_specs=pl.BlockSpec((blk, HEAD_DIM), lambda b, h, i: (b * nb + i, h)),
        out_shape=jax.ShapeDtypeStruct((batch * seq, HEADS * HEAD_DIM), BF16),
        compiler_params=_params(("parallel", "parallel", "arbitrary")),
        name="moba_prompt_attn",
    )(qm, kext, vb, means, tiles)


def _mla_sample_kernel(pt_ref, qabs_ref, qpe_ref, cnew_ref, kpenew_ref, wkt_ref, lat_hbm, kpet_hbm, o_ref,
                       cbuf, kbuf, cb16, s_scr, sem, *, n_pages, tile, unroll):
    s_idx = pl.program_id(0)
    n_seq = pl.num_programs(0)
    slot = s_idx % 2
    past = n_pages * PAGE_SIZE
    dec = cnew_ref.shape[0]
    rows = qabs_ref.shape[0]

    def copies(seq, sl, p):
        page = pt_ref[seq * n_pages + p]
        dst = pl.ds(pl.multiple_of(p * PAGE_SIZE, PAGE_SIZE), PAGE_SIZE)
        return (pltpu.make_async_copy(lat_hbm.at[page], cbuf.at[sl, dst], sem.at[0, sl]),
                pltpu.make_async_copy(kpet_hbm.at[page], kbuf.at[sl, :, dst], sem.at[1, sl]))

    def for_pages(seq, sl, fn):
        def body(i, _):
            for u in range(unroll):
                for cp in copies(seq, sl, i * unroll + u):
                    fn(cp)
            return 0
        lax.fori_loop(0, n_pages // unroll, body, 0)

    @pl.when(s_idx == 0)
    def _():
        for_pages(s_idx, slot, lambda cp: cp.start())

    @pl.when(s_idx + 1 < n_seq)
    def _():
        for_pages(s_idx + 1, 1 - slot, lambda cp: cp.start())

    for_pages(s_idx, slot, lambda cp: cp.wait())

    qabs = qabs_ref[...]
    qpe = qpe_ref[...]
    wkt = wkt_ref[...]

    def scores(c_t, kpe_t):
        kn_t = _dot_nt(wkt, c_t)
        ssq = [jnp.sum(jnp.square(kn_t[h * HEAD_DIM:(h + 1) * HEAD_DIM, :]), axis=0, keepdims=True)
               for h in range(HEADS)]
        inv = lax.rsqrt(jnp.concatenate(ssq, axis=0) * (1.0 / HEAD_DIM) + EPS)
        inv = jnp.concatenate([inv] * dec, axis=0)
        return _dot_nt(qabs, c_t) * inv + _dot(qpe, kpe_t)

    n_tiles = past // tile

    def score_tile(t, _):
        t0 = pl.multiple_of(t * tile, tile)
        c_t = cbuf[slot, pl.ds(t0, tile), :].astype(BF16)
        cb16[pl.ds(t0, tile), :] = c_t
        s_scr[:, pl.ds(t0, tile)] = scores(c_t, kbuf[slot, :, pl.ds(t0, tile)].astype(BF16))
        return 0

    lax.fori_loop(0, n_tiles, score_tile, 0)
    c_new = jnp.concatenate([cnew_ref[...], jnp.zeros((PAGE_SIZE - dec, KV_LORA), F32)], axis=0).astype(BF16)
    s_new = scores(c_new, kpenew_ref[...].astype(BF16))
    qrow = lax.broadcasted_iota(jnp.int32, (rows, PAGE_SIZE), 0) // HEADS
    lane = lax.broadcasted_iota(jnp.int32, (rows, PAGE_SIZE), 1)
    s_scr[:, pl.ds(past, PAGE_SIZE)] = jnp.where(lane <= qrow, s_new, NEG)

    s_all = s_scr[...]
    m = jnp.max(s_all, axis=-1, keepdims=True)
    p_all = jnp.exp2(s_all - m)
    l = jnp.sum(p_all, axis=-1, keepdims=True)
    s_scr[...] = p_all

    def pv_tile(t, acc):
        t0 = pl.multiple_of(t * tile, tile)
        return acc + _dot(s_scr[:, pl.ds(t0, tile)].astype(BF16), cb16[pl.ds(t0, tile), :])

    acc = lax.fori_loop(0, n_tiles, pv_tile, _dot(s_scr[:, pl.ds(past, PAGE_SIZE)].astype(BF16), c_new))
    o_ref[...] = acc / l


def _largest_tile(total, unit, cap):
    n = total // unit
    best = 1
    for d in range(1, n + 1):
        if n % d == 0 and d * unit <= cap:
            best = d
    return best * unit


def _mla_sample(page_table, qabs, qpe, c_new, kpe_new_t, wkt, lat, kpe_cache_t):
    db, n_pages = page_table.shape
    dec = c_new.shape[1]
    rows = dec * HEADS
    past = n_pages * PAGE_SIZE
    tile = _largest_tile(past, PAGE_SIZE, 1024)
    unroll = math.gcd(n_pages, 4)
    per_seq = lambda width: pl.BlockSpec((None, rows, width), lambda s, pt: (s, 0, 0))
    return pl.pallas_call(
        functools.partial(_mla_sample_kernel, n_pages=n_pages, tile=tile, unroll=unroll),
        grid_spec=pltpu.PrefetchScalarGridSpec(
            num_scalar_prefetch=1,
            grid=(db,),
            in_specs=[per_seq(KV_LORA), per_seq(ROPE_DIM),
                      pl.BlockSpec((None, dec, KV_LORA), lambda s, pt: (s, 0, 0)),
                      pl.BlockSpec((None, ROPE_DIM, PAGE_SIZE), lambda s, pt: (s, 0, 0)),
                      pl.BlockSpec(wkt.shape, lambda s, pt: (0, 0)),
                      pl.BlockSpec(memory_space=pl.ANY),
                      pl.BlockSpec(memory_space=pl.ANY)],
            out_specs=per_seq(KV_LORA),
            scratch_shapes=[pltpu.VMEM((2, past, KV_LORA), F32),
                            pltpu.VMEM((2, ROPE_DIM, past), F32),
                            pltpu.VMEM((past, KV_LORA), BF16),
                            pltpu.VMEM((rows, past + PAGE_SIZE), F32),
                            pltpu.SemaphoreType.DMA((2, 2))]),
        out_shape=jax.ShapeDtypeStruct((db, rows, KV_LORA), F32),
        compiler_params=_params(("arbitrary",)),
        name="mla_sample_attn",
    )(page_table.reshape(-1), qabs, qpe, c_new, kpe_new_t, wkt, lat, kpe_cache_t)


def _mla_out_kernel(x_ref, wv_ref, o_ref):
    for h in range(HEADS):
        hi, lo = _split(x_ref[:, h * KV_LORA:(h + 1) * KV_LORA])
        wv = wv_ref[:, h * HEAD_DIM:(h + 1) * HEAD_DIM]
        o_ref[:, h * HEAD_DIM:(h + 1) * HEAD_DIM] = (_dot(hi, wv) + _dot(lo, wv)).astype(o_ref.dtype)


def _mla_out(x, wv):
    t = x.shape[0]
    return pl.pallas_call(
        _mla_out_kernel,
        grid=(1,),
        in_specs=[pl.BlockSpec(x.shape, lambda i: (0, 0)), pl.BlockSpec(wv.shape, lambda i: (0, 0))],
        out_specs=pl.BlockSpec((t, HEADS * HEAD_DIM), lambda i: (0, 0)),
        out_shape=jax.ShapeDtypeStruct((t, HEADS * HEAD_DIM), BF16),
        compiler_params=_params(("arbitrary",)),
        name="mla_sample_out",
    )(x, wv)


SELECT_DEPTH = 3


def _moba_select_kernel(pt_ref, q_ref, k_hbm, idx_ref, kbuf, sums, sem, *, n_pages, chunk_pages):
    g = pl.program_id(0)
    n_steps = pl.num_programs(0)
    n_chunks = n_pages // chunk_pages
    chunk = g % n_chunks
    slot = g % SELECT_DEPTH
    n_full = n_pages // PAGES_PER_BLOCK
    blocks_per_chunk = chunk_pages // PAGES_PER_BLOCK
    rows = q_ref.shape[0]
    width = sums.shape[0]

    def copy(step, sl, p):
        page = pt_ref[step * chunk_pages + p]
        return pltpu.make_async_copy(k_hbm.at[page], kbuf.at[sl, p], sem.at[sl])

    def start_fetch(step, sl):
        for p in range(chunk_pages):
            copy(step, sl, p).start()

    @pl.when(g == 0)
    def _():
        sums[...] = jnp.zeros(sums.shape, F32)
        for d in range(SELECT_DEPTH - 1):
            start_fetch(d, d)

    ahead = g + SELECT_DEPTH - 1

    @pl.when(ahead < n_steps)
    def _():
        start_fetch(ahead, ahead % SELECT_DEPTH)

    for p in range(chunk_pages):
        copy(g, slot, p).wait()

    for b in range(blocks_per_chunk):
        x = kbuf[slot, b * PAGES_PER_BLOCK:(b + 1) * PAGES_PER_BLOCK]
        n = chunk * blocks_per_chunk + b
        sums[pl.ds(pl.multiple_of(n * HEADS, HEADS), HEADS), :] = jnp.sum(x, axis=(0, 1))

    @pl.when(chunk == n_chunks - 1)
    def _():
        qh, ql = _split(q_ref[...])
        mh, ml = _split(sums[...] * (1.0 / MOBA_BLOCK))
        gate = _dot_nt(qh, mh) + _dot_nt(qh, ml) + _dot_nt(ql, mh)
        col = lax.broadcasted_iota(jnp.int32, (rows, width), 1)
        row = lax.broadcasted_iota(jnp.int32, (rows, width), 0)
        valid = ((col & (HEADS - 1)) == (row & (HEADS - 1))) & (col < n_full * HEADS)
        gate = jnp.where(valid, gate, NEG)
        rank = jnp.zeros((rows, width), F32)
        for s in range(1, width // HEADS):
            other = pltpu.roll(gate, s * HEADS, axis=1)
            rank = rank + jnp.where(col >= s * HEADS, jnp.where(other >= gate, 1.0, 0.0), jnp.where(other > gate, 1.0, 0.0))
        blk = (col // HEADS).astype(F32)
        lane = lax.broadcasted_iota(jnp.int32, (rows, 128), 1)
        out = jnp.zeros((rows, 128), F32)
        for k in range(MOBA_TOPK):
            pick = jnp.sum(jnp.where(valid & (rank == k), blk, 0.0), axis=-1, keepdims=True)
            out = jnp.where(lane == k, pick, out)
        idx_ref[...] = out.astype(jnp.int32)


def _moba_select(page_table, q, k4):
    db, n_pages = page_table.shape
    rows = q.shape[1]
    chunk_pages = 8 if n_pages % 8 == 0 else PAGES_PER_BLOCK
    n_chunks = n_pages // chunk_pages
    n_full = n_pages // PAGES_PER_BLOCK
    width = -(-n_full * HEADS // 128) * 128
    assert db * n_chunks >= SELECT_DEPTH - 1
    return pl.pallas_call(
        functools.partial(_moba_select_kernel, n_pages=n_pages, chunk_pages=chunk_pages),
        grid_spec=pltpu.PrefetchScalarGridSpec(
            num_scalar_prefetch=1,
            grid=(db * n_chunks,),
            in_specs=[pl.BlockSpec((None, rows, HEAD_DIM), lambda g, pt: (g // n_chunks, 0, 0)),
                      pl.BlockSpec(memory_space=pl.ANY)],
            out_specs=pl.BlockSpec((None, rows, 128), lambda g, pt: (g // n_chunks, 0, 0)),
            scratch_shapes=[pltpu.VMEM((SELECT_DEPTH, chunk_pages, PAGE_SIZE, HEADS, HEAD_DIM), F32),
                            pltpu.VMEM((width, HEAD_DIM), F32),
                            pltpu.SemaphoreType.DMA((SELECT_DEPTH,))]),
        out_shape=jax.ShapeDtypeStruct((db, rows, 128), jnp.int32),
        compiler_params=_params(("arbitrary",)),
        name="moba_sample_select",
    )(page_table.reshape(-1), q, k4)


def _moba_sample_kernel(pt_ref, idx_ref, q_ref, kn_ref, vn_ref, last_ref, new_ref, k_hbm, v_hbm, o_ref,
                        kbuf, vbuf, sem, *, n_pages, dec):
    s_idx = pl.program_id(0)
    h = pl.program_id(1)
    g = s_idx * HEADS + h
    n_steps = pl.num_programs(0) * HEADS
    slot = g % 2
    n_full = n_pages // PAGES_PER_BLOCK
    n_sel = MOBA_TOPK * dec

    def copies(step, sl, q, k, j):
        seq = step // HEADS
        head = step % HEADS
        blk = idx_ref[step * n_sel + q * MOBA_TOPK + k]
        page = pt_ref[seq * n_pages + blk * PAGES_PER_BLOCK + j]
        dst = pl.ds((k * PAGES_PER_BLOCK + j) * PAGE_SIZE, PAGE_SIZE)
        return (pltpu.make_async_copy(k_hbm.at[page, :, head, :], kbuf.at[sl, q, dst], sem.at[0, sl]),
                pltpu.make_async_copy(v_hbm.at[page, :, head, :], vbuf.at[sl, q, dst], sem.at[1, sl]))

    def for_all(step, sl, fn):
        for q in range(dec):
            for k in range(MOBA_TOPK):
                for j in range(PAGES_PER_BLOCK):
                    for cp in copies(step, sl, q, k, j):
                        fn(cp)

    @pl.when(g == 0)
    def _():
        for_all(g, slot, lambda cp: cp.start())

    @pl.when(g + 1 < n_steps)
    def _():
        for_all(g + 1, 1 - slot, lambda cp: cp.start())

    for_all(g, slot, lambda cp: cp.wait())

    seg = MOBA_TOPK * MOBA_BLOCK
    total = dec * seg
    q8 = (q_ref[...] * (MOBA_SCALE * LOG2E)).astype(BF16)
    kn = kn_ref[...].astype(BF16)
    vn = vn_ref[...].astype(BF16)
    k_all = kbuf[slot].reshape(total, HEAD_DIM).astype(BF16)
    v_all = vbuf[slot].reshape(total, HEAD_DIM).astype(BF16)
    last = last_ref[...]
    bias = [jnp.where(idx_ref[g * n_sel + i] == n_full - 1, last, 0.0) for i in range(n_sel)]
    row = lax.broadcasted_iota(jnp.int32, (8, total), 0)
    col = lax.broadcasted_iota(jnp.int32, (8, total), 1)
    own = (col >= row * seg) & (col < (row + 1) * seg)
    s = jnp.where(own, _dot_nt(q8, k_all) + jnp.concatenate(bias, axis=-1), NEG)
    sn = _dot_nt(q8, kn) + new_ref[:, :8]
    m = jnp.maximum(jnp.max(s, axis=-1, keepdims=True), jnp.max(sn, axis=-1, keepdims=True))
    p = jnp.exp2(s - m)
    pn = jnp.exp2(sn - m)
    l = jnp.sum(p, axis=-1, keepdims=True) + jnp.sum(pn, axis=-1, keepdims=True)
    o = (_dot(p.astype(BF16), v_all) + _dot(pn.astype(BF16), vn)) / l
    o_ref[...] = o[:dec, :]


def _moba_sample(page_table, idx, q8, kn8, vn8, bias_last, bias_new, k4, v4, dec):
    db, n_pages = page_table.shape
    per = lambda: pl.BlockSpec((None, None, 8, HEAD_DIM), lambda s, h, pt, ix: (s, h, 0, 0))
    return pl.pallas_call(
        functools.partial(_moba_sample_kernel, n_pages=n_pages, dec=dec),
        grid_spec=pltpu.PrefetchScalarGridSpec(
            num_scalar_prefetch=2,
            grid=(db, HEADS),
            in_specs=[per(), per(), per(),
                      pl.BlockSpec((None, 8, MOBA_BLOCK), lambda s, h, pt, ix: (h, 0, 0)),
                      pl.BlockSpec((None, 8, 128), lambda s, h, pt, ix: (h, 0, 0)),
                      pl.BlockSpec(memory_space=pl.ANY),
                      pl.BlockSpec(memory_space=pl.ANY)],
            out_specs=pl.BlockSpec((None, dec, HEAD_DIM), lambda s, h, pt, ix: (s, 0, h)),
            scratch_shapes=[pltpu.VMEM((2, dec, MOBA_TOPK * MOBA_BLOCK, HEAD_DIM), F32),
                            pltpu.VMEM((2, dec, MOBA_TOPK * MOBA_BLOCK, HEAD_DIM), F32),
                            pltpu.SemaphoreType.DMA((2, 2))]),
        out_shape=jax.ShapeDtypeStruct((db, dec, HEADS * HEAD_DIM), F32),
        compiler_params=_params(("arbitrary", "arbitrary")),
        name="moba_sample_attn",
    )(page_table.reshape(-1), idx, q8, kn8, vn8, bias_last, bias_new, k4, v4)


def _merge_ffn_kernel(x_ref, oa_ref, ob_ref, ga_ref, wg_ref, wo_ref, gf_ref, wgu_ref, wd_ref, y_ref):
    x = x_ref[...]
    xn = _rms(x, ga_ref[...]).astype(BF16)
    g = _dot(xn, wg_ref[...])
    mix = jax.nn.sigmoid(g[:, :D_MODEL]) * oa_ref[...].astype(F32) + jax.nn.sigmoid(g[:, D_MODEL:]) * ob_ref[...].astype(F32)
    hres = x + _dot(mix.astype(BF16), wo_ref[...])
    gu = _dot(_rms(hres, gf_ref[...]).astype(BF16), wgu_ref[...])
    d_ff = wd_ref.shape[0]
    act = jax.nn.silu(gu[:, :d_ff]) * gu[:, d_ff:]
    y_ref[...] = hres + _dot(act.astype(BF16), wd_ref[...])


def _merge_ffn(x, o_a, o_b, w):
    t = x.shape[0]
    tm = ROW_TILE
    row = lambda: pl.BlockSpec((tm, D_MODEL), lambda i: (i, 0))
    consts = [w["norm_attn"], w["w_g"], w["w_o"], w["norm_ffn"], w["w_gate_up"], w["w_down"]]
    return pl.pallas_call(
        _merge_ffn_kernel,
        grid=(t // tm,),
        in_specs=[row(), row(), row()] + [_const_spec(a.shape) for a in consts],
        out_specs=row(),
        out_shape=jax.ShapeDtypeStruct((t, D_MODEL), F32),
        compiler_params=_params(("parallel",)),
        name="merge_ffn",
    )(x, o_a, o_b, *consts)


def _rope_tables(pos):
    half = ROPE_DIM // 2
    freqs = ROPE_THETA ** (-jnp.arange(half, dtype=F32) / half)
    ang = pos.astype(F32)[:, None] * freqs
    cos, sin = jnp.cos(ang), jnp.sin(ang)
    return jnp.concatenate([cos, cos], axis=-1), jnp.concatenate([-sin, sin], axis=-1)


def _layer_weights(l, norm_attn, w_in, q_a_norm, w_q_b, kv_a_norm, w_kv_b, mla_qn_gain, mla_qr_gain, mla_kn_gain,
                   mla_kr_gain, moba_q_gain, moba_k_gain, w_o, norm_ffn, w_gate_up, w_down):
    width = HEADS * HEAD_DIM
    a_end = Q_LORA + KV_LORA + ROPE_DIM
    row = lambda a: a[l].reshape(1, -1)
    win = w_in[l].astype(BF16)
    wqb = w_q_b[l].reshape(Q_LORA, HEADS, HEAD_DIM + ROPE_DIM)
    wkv = w_kv_b[l].reshape(KV_LORA, HEADS, 2 * HEAD_DIM)
    return {
        "norm_attn": row(norm_attn), "q_a_norm": row(q_a_norm), "kv_a_norm": row(kv_a_norm),
        "qn_gain": row(mla_qn_gain), "qr_gain": row(mla_qr_gain), "kn_gain": row(mla_kn_gain),
        "kr_gain": row(mla_kr_gain), "moba_q_gain": row(moba_q_gain), "moba_k_gain": row(moba_k_gain),
        "norm_ffn": row(norm_ffn),
        "w_a": win[:, :a_end], "w_m": win[:, a_end:a_end + 3 * width], "w_g": win[:, a_end + 3 * width:],
        "w_q_b": jnp.concatenate([wqb[:, :, :HEAD_DIM].reshape(Q_LORA, -1),
                                  wqb[:, :, HEAD_DIM:].reshape(Q_LORA, -1)], axis=-1).astype(BF16),
        "w_kv": jnp.concatenate([wkv[:, :, :HEAD_DIM].reshape(KV_LORA, -1),
                                 wkv[:, :, HEAD_DIM:].reshape(KV_LORA, -1)], axis=-1).astype(BF16),
        "w_o": w_o[l].astype(BF16), "w_gate_up": w_gate_up[l].astype(BF16), "w_down": w_down[l].astype(BF16),
    }


def kernel(x_prompt, x_sample, cache_mla_latent, cache_mla_kpe, cache_moba_k, cache_moba_v, page_table, norm_attn, w_in, q_a_norm, w_q_b, kv_a_norm, w_kv_b, mla_qn_gain, mla_qr_gain, mla_kn_gain, mla_kr_gain, moba_q_gain, moba_k_gain, t5_table, w_o, norm_ffn, w_gate_up, w_down):
    batch, seq, _ = x_prompt.shape
    db, dec, _ = x_sample.shape
    depth = w_in.shape[0]
    n_pool = cache_mla_latent.shape[1]
    n_pages = page_table.shape[1]
    past = n_pages * PAGE_SIZE
    width = HEADS * HEAD_DIM
    assert seq % ROW_TILE == 0 and (db * dec) % ROW_TILE == 0 and ROW_TILE % dec == 0
    assert past % MOBA_BLOCK == 0 and past // MOBA_BLOCK >= MOBA_TOPK and dec <= 8

    cos_p, sin_p = _rope_tables(jnp.arange(seq))
    cos_s, sin_s = _rope_tables(past + jnp.arange(ROW_TILE) % dec)
    tiles, bias_last, bias_new = _t5_tables(t5_table, 8)

    yp = x_prompt.reshape(batch * seq, D_MODEL)
    ys = x_sample.reshape(db * dec, D_MODEL)
    outs = [[] for _ in range(8)]
    for l in range(depth):
        w = _layer_weights(l, norm_attn, w_in, q_a_norm, w_q_b, kv_a_norm, w_kv_b, mla_qn_gain, mla_qr_gain,
                           mla_kn_gain, mla_kr_gain, moba_q_gain, moba_k_gain, w_o, norm_ffn, w_gate_up, w_down)
        c, kpe, qcat, kcat, v = _mla_proj(yp, w, cos_p, sin_p, sample=False)
        qm, km, vm, kext, vb, means = _moba_proj(yp, w, blocks_per_seq=seq // MOBA_BLOCK)
        o_a = _mla_prompt(qcat, kcat, v, batch, seq)
        o_b = _moba_prompt(qm, kext, vb, means.reshape(batch, seq // MOBA_BLOCK, width), tiles, batch, seq)
        outs[0].append(c.reshape(batch, seq, KV_LORA))
        outs[1].append(kpe.reshape(batch, seq, ROPE_DIM))
        outs[2].append(km.reshape(batch, seq, HEADS, HEAD_DIM))
        outs[3].append(vm.reshape(batch, seq, HEADS, HEAD_DIM))
        yp = _merge_ffn(yp, o_a, o_b, w)
        c, kpe, qabs, qpe = _mla_proj(ys, w, cos_s, sin_s, sample=True)
        qm, km, vm = _moba_proj(ys, w)
        kpe_new_t = jnp.pad(jnp.swapaxes(kpe.reshape(db, dec, ROPE_DIM), 1, 2), ((0, 0), (0, 0), (0, PAGE_SIZE - dec)))
        lat = _mla_sample(page_table, qabs.reshape(db, dec * HEADS, KV_LORA), qpe.reshape(db, dec * HEADS, ROPE_DIM),
                          c.reshape(db, dec, KV_LORA), kpe_new_t,
                          w["w_kv"][:, :width].T, cache_mla_latent[l], jnp.swapaxes(cache_mla_kpe[l], 1, 2))
        o_a = _mla_out(lat.reshape(db * dec, HEADS * KV_LORA), w["w_kv"][:, width:])
        k4 = cache_moba_k[l]
        v4 = cache_moba_v[l]
        idx = _moba_select(page_table, qm.reshape(db, dec * HEADS, HEAD_DIM), k4)
        idx = jnp.transpose(idx[:, :, :MOBA_TOPK].reshape(db, dec, HEADS, MOBA_TOPK), (0, 2, 1, 3)).reshape(-1)
        to_heads = lambda a: jnp.pad(jnp.transpose(a.reshape(db, dec, HEADS, HEAD_DIM), (0, 2, 1, 3)),
                                     ((0, 0), (0, 0), (0, 8 - dec), (0, 0)))
        o_b = _moba_sample(page_table, idx, to_heads(qm), to_heads(km), to_heads(vm), bias_last, bias_new, k4, v4, dec)
        outs[4].append(c.reshape(db, dec, KV_LORA))
        outs[5].append(kpe.reshape(db, dec, ROPE_DIM))
        outs[6].append(km.reshape(db, dec, HEADS, HEAD_DIM))
        outs[7].append(vm.reshape(db, dec, HEADS, HEAD_DIM))
        ys = _merge_ffn(ys, o_a, o_b.reshape(db * dec, width), w)
    return (yp.reshape(batch, seq, D_MODEL), ys.reshape(db, dec, D_MODEL)) + tuple(jnp.stack(o, axis=0) for o in outs)
```

```python
import functools
import math

import numpy as np
import jax
import jax.numpy as jnp
from jax import lax
from jax.experimental import pallas as pl
from jax.experimental.pallas import tpu as pltpu

F32 = jnp.float32
BF16 = jnp.bfloat16

D_MODEL = 1024
HEADS = 8
HEAD_DIM = 128
ROPE_DIM = 64
Q_LORA = 256
KV_LORA = 256
QK_PAD = 256
ROPE_THETA = 10000.0
MLA_SCALE = (HEAD_DIM + ROPE_DIM) ** -0.5
MOBA_BLOCK = 256
MOBA_TOPK = 3
MOBA_SCALE = HEAD_DIM ** -0.5
PAGE_SIZE = 128
PAGES_PER_BLOCK = MOBA_BLOCK // PAGE_SIZE
T5_BUCKETS = 32
T5_MAX_DIST = 128
EPS = 1e-6
NEG = -1e30
LOG2E = math.log2(math.e)
ROW_TILE = 256
MLA_TILE = 512
VMEM_LIMIT = 56 * 1024 * 1024

_NT = (((1,), (1,)), ((), ()))


def _dot(a, b):
    return jnp.dot(a, b, preferred_element_type=F32)


def _dot_nt(a, b):
    return lax.dot_general(a, b, _NT, preferred_element_type=F32)


def _split(x):
    hi = x.astype(BF16)
    lo = (x - hi.astype(F32)).astype(BF16)
    return hi, lo


def _rms(x, g):
    ms = jnp.mean(x * x, axis=-1, keepdims=True)
    return x * lax.rsqrt(ms + EPS) * g


def _rope(x, cosf, sinf):
    half = ROPE_DIM // 2
    swapped = jnp.concatenate([x[:, half:], x[:, :half]], axis=-1)
    return x * cosf + swapped * sinf


def _const_spec(shape):
    n = len(shape)
    return pl.BlockSpec(shape, lambda *_: (0,) * n, pipeline_mode=pl.Buffered(1))


def _params(sem):
    return pltpu.CompilerParams(dimension_semantics=sem, vmem_limit_bytes=VMEM_LIMIT)


def _t5_bucket_upper_bounds(max_dist):
    d = np.arange(max_dist + 1)
    max_exact = T5_BUCKETS // 2
    dd = np.maximum(d, max_exact).astype(np.float32)
    large = max_exact + (np.log(dd / np.float32(max_exact)) / np.float32(math.log(T5_MAX_DIST / max_exact))
                         * np.float32(T5_BUCKETS - max_exact)).astype(np.int32)
    bucket = np.where(d < max_exact, d, np.minimum(large, T5_BUCKETS - 1))
    assert bucket[-1] == T5_BUCKETS - 1
    return [int(d[bucket == b].max()) if np.any(bucket == b) else None for b in range(T5_BUCKETS)]


def _t5_bias(dist, h, t5_ref, bounds):
    res = jnp.full(dist.shape, t5_ref[T5_BUCKETS - 1, h], F32)
    for b in range(T5_BUCKETS - 2, -1, -1):
        if bounds[b] is not None:
            res = jnp.where(dist <= bounds[b], t5_ref[b, h], res)
    return res


def _t5_kernel(t5_ref, tiles_ref, last_ref, new_ref, *, bounds, dec_seq):
    h = pl.program_id(0)
    far = t5_ref[T5_BUCKETS - 1, h]
    rel = lambda dist: (_t5_bias(dist, h, t5_ref, bounds) - far) * LOG2E
    r = lax.broadcasted_iota(jnp.int32, (MOBA_BLOCK, MOBA_BLOCK), 0)
    c = lax.broadcasted_iota(jnp.int32, (MOBA_BLOCK, MOBA_BLOCK), 1)
    tiles_ref[0] = jnp.where(c <= r, rel(r - c), NEG)
    tiles_ref[1] = rel(MOBA_BLOCK + r - c)
    q = lax.broadcasted_iota(jnp.int32, (dec_seq, MOBA_BLOCK), 0)
    j = lax.broadcasted_iota(jnp.int32, (dec_seq, MOBA_BLOCK), 1)
    last_ref[...] = rel(MOBA_BLOCK + q - j)
    qn = lax.broadcasted_iota(jnp.int32, (dec_seq, 128), 0)
    jn = lax.broadcasted_iota(jnp.int32, (dec_seq, 128), 1)
    new_ref[...] = jnp.where(jn <= qn, rel(qn - jn), NEG)


def _t5_tables(t5_table, dec_seq):
    bounds = _t5_bucket_upper_bounds(2 * MOBA_BLOCK + dec_seq)
    return pl.pallas_call(
        functools.partial(_t5_kernel, bounds=bounds, dec_seq=dec_seq),
        grid=(HEADS,),
        in_specs=[pl.BlockSpec(memory_space=pltpu.SMEM)],
        out_specs=[pl.BlockSpec((None, 2, MOBA_BLOCK, MOBA_BLOCK), lambda h: (h, 0, 0, 0)),
                   pl.BlockSpec((None, dec_seq, MOBA_BLOCK), lambda h: (h, 0, 0)),
                   pl.BlockSpec((None, dec_seq, 128), lambda h: (h, 0, 0))],
        out_shape=[jax.ShapeDtypeStruct((HEADS, 2, MOBA_BLOCK, MOBA_BLOCK), F32),
                   jax.ShapeDtypeStruct((HEADS, dec_seq, MOBA_BLOCK), F32),
                   jax.ShapeDtypeStruct((HEADS, dec_seq, 128), F32)],
        compiler_params=_params(("arbitrary",)),
        name="t5_bias",
    )(t5_table)


def _mla_proj_kernel(x_ref, g_ref, wa_ref, qan_ref, wqb_ref, kvn_ref, qng_ref, qrg_ref, krg_ref, kng_ref,
                     cos_ref, sin_ref, wkv_ref, c_ref, kpe_ref, *outs, sample):
    tm = x_ref.shape[0]
    xn = _rms(x_ref[...], g_ref[...]).astype(BF16)
    pa = _dot(xn, wa_ref[...])
    cosf, sinf = cos_ref[...], sin_ref[...]
    c = _rms(pa[:, Q_LORA:Q_LORA + KV_LORA], kvn_ref[...])
    kpe = _rope(_rms(pa[:, Q_LORA + KV_LORA:], krg_ref[...]), cosf, sinf)
    c_ref[...] = c
    kpe_ref[...] = kpe
    q = _dot(_rms(pa[:, :Q_LORA], qan_ref[...]).astype(BF16), wqb_ref[...])
    nope_w = HEADS * HEAD_DIM
    pad = jnp.zeros((tm, QK_PAD - HEAD_DIM - ROPE_DIM), F32)
    if sample:
        qabs_ref, qpe_ref = outs
        qpes = []
    else:
        qcat_ref, kcat_ref, v_ref = outs
        kv = _dot(c.astype(BF16), wkv_ref[...])
        v_ref[...] = kv[:, nope_w:].astype(BF16)
    for h in range(HEADS):
        qn = _rms(q[:, h * HEAD_DIM:(h + 1) * HEAD_DIM], qng_ref[...]) * (MLA_SCALE * LOG2E)
        qp = _rope(_rms(q[:, nope_w + h * ROPE_DIM:nope_w + (h + 1) * ROPE_DIM], qrg_ref[...]), cosf, sinf) * (MLA_SCALE * LOG2E)
        if sample:
            hi, lo = _split(qn * kng_ref[...])
            wk = wkv_ref[:, h * HEAD_DIM:(h + 1) * HEAD_DIM]
            qabs_ref[:, h * KV_LORA:(h + 1) * KV_LORA] = (_dot_nt(hi, wk) + _dot_nt(lo, wk)).astype(BF16)
            qpes.append(qp)
        else:
            kn = _rms(kv[:, h * HEAD_DIM:(h + 1) * HEAD_DIM], kng_ref[...])
            qcat_ref[:, h * QK_PAD:(h + 1) * QK_PAD] = jnp.concatenate([qn, qp, pad], axis=-1).astype(BF16)
            kcat_ref[:, h * QK_PAD:(h + 1) * QK_PAD] = jnp.concatenate([kn, kpe, pad], axis=-1).astype(BF16)
    if sample:
        qpe_ref[...] = jnp.concatenate(qpes, axis=-1).astype(BF16)


def _mla_proj(x, w, cosf, sinf, *, sample):
    t = x.shape[0]
    tm = ROW_TILE
    n_pos = cosf.shape[0] // tm
    row = lambda width: pl.BlockSpec((tm, width), lambda i: (i, 0))
    pos = pl.BlockSpec((tm, ROPE_DIM), lambda i: (i % n_pos, 0))
    consts = [w["norm_attn"], w["w_a"], w["q_a_norm"], w["w_q_b"], w["kv_a_norm"], w["qn_gain"], w["qr_gain"],
              w["kr_gain"], w["kn_gain"]]
    if sample:
        widths, dtypes = [KV_LORA, ROPE_DIM, HEADS * KV_LORA, HEADS * ROPE_DIM], [F32, F32, BF16, BF16]
    else:
        widths, dtypes = [KV_LORA, ROPE_DIM, HEADS * QK_PAD, HEADS * QK_PAD, HEADS * HEAD_DIM], [F32, F32, BF16, BF16, BF16]
    return pl.pallas_call(
        functools.partial(_mla_proj_kernel, sample=sample),
        grid=(t // tm,),
        in_specs=[row(D_MODEL)] + [_const_spec(a.shape) for a in consts] + [pos, pos, _const_spec(w["w_kv"].shape)],
        out_specs=[row(n) for n in widths],
        out_shape=[jax.ShapeDtypeStruct((t, n), d) for n, d in zip(widths, dtypes)],
        compiler_params=_params(("parallel",)),
        name="mla_proj_sample" if sample else "mla_proj_prompt",
    )(x, *consts, cosf, sinf, w["w_kv"])


def _moba_proj_kernel(x_ref, g_ref, wm_ref, qg_ref, kg_ref, q_ref, k_ref, v_ref, *prompt_refs, blocks_per_seq):
    tm = x_ref.shape[0]
    xn = _rms(x_ref[...], g_ref[...]).astype(BF16)
    pm = _dot(xn, wm_ref[...])
    w = HEADS * HEAD_DIM
    v = pm[:, 2 * w:]
    v_ref[...] = v
    if prompt_refs:
        kext_ref, vb_ref, mean_ref = prompt_refs
        vb_ref[...] = v.astype(BF16)
        lane = lax.broadcasted_iota(jnp.int32, (tm, QK_PAD - HEAD_DIM), 1)
        onehot = jnp.where(lane == pl.program_id(0) % blocks_per_seq, 1.0, 0.0).astype(BF16)
    for h in range(HEADS):
        sl = slice(h * HEAD_DIM, (h + 1) * HEAD_DIM)
        q_ref[:, sl] = _rms(pm[:, h * HEAD_DIM:(h + 1) * HEAD_DIM], qg_ref[...])
        k = _rms(pm[:, w + h * HEAD_DIM:w + (h + 1) * HEAD_DIM], kg_ref[...])
        k_ref[:, sl] = k
        if prompt_refs:
            kext_ref[:, h * QK_PAD:(h + 1) * QK_PAD] = jnp.concatenate([k.astype(BF16), onehot], axis=-1)
            mean_ref[:, sl] = jnp.mean(k, axis=0, keepdims=True)


def _moba_proj(x, w, *, blocks_per_seq=None):
    t = x.shape[0]
    tm = ROW_TILE
    width = HEADS * HEAD_DIM
    prompt = blocks_per_seq is not None
    row = lambda n=width: pl.BlockSpec((tm, n), lambda i: (i, 0))
    consts = [w["norm_attn"], w["w_m"], w["moba_q_gain"], w["moba_k_gain"]]
    out_specs = [row() for _ in range(3)]
    out_shape = [jax.ShapeDtypeStruct((t, width), F32) for _ in range(3)]
    if prompt:
        assert tm == MOBA_BLOCK and blocks_per_seq <= QK_PAD - HEAD_DIM
        out_specs += [row(HEADS * QK_PAD), row(), pl.BlockSpec((None, 1, width), lambda i: (i, 0, 0))]
        out_shape += [jax.ShapeDtypeStruct((t, HEADS * QK_PAD), BF16), jax.ShapeDtypeStruct((t, width), BF16),
                      jax.ShapeDtypeStruct((t // tm, 1, width), F32)]
    return pl.pallas_call(
        functools.partial(_moba_proj_kernel, blocks_per_seq=blocks_per_seq),
        grid=(t // tm,),
        in_specs=[pl.BlockSpec((tm, D_MODEL), lambda i: (i, 0))] + [_const_spec(a.shape) for a in consts],
        out_specs=out_specs,
        out_shape=out_shape,
        compiler_params=_params(("parallel",)),
        name="moba_proj_prompt" if prompt else "moba_proj_sample",
    )(x, *consts)


def _online_softmax_step(s, v, m, l, acc):
    m_new = jnp.maximum(m, jnp.max(s, axis=-1, keepdims=True))
    a = jnp.exp2(m - m_new)
    p = jnp.exp2(s - m_new)
    return m_new, a * l + jnp.sum(p, axis=-1, keepdims=True), a * acc + _dot(p.astype(BF16), v)


def _mla_prompt_kernel(q_ref, k_ref, v_ref, o_ref):
    qi = pl.program_id(2)
    tq = q_ref.shape[0]
    q = q_ref[...]
    d0 = pl.multiple_of(qi * tq, tq)
    s = _dot_nt(q, k_ref[pl.ds(d0, tq), :])
    r = lax.broadcasted_iota(jnp.int32, (tq, tq), 0)
    c = lax.broadcasted_iota(jnp.int32, (tq, tq), 1)
    s = jnp.where(c <= r, s, NEG)
    m = jnp.max(s, axis=-1, keepdims=True)
    p = jnp.exp2(s - m)
    l = jnp.sum(p, axis=-1, keepdims=True)
    acc = _dot(p.astype(BF16), v_ref[pl.ds(d0, tq), :])

    def body(j, carry):
        j0 = pl.multiple_of(j * tq, tq)
        return _online_softmax_step(_dot_nt(q, k_ref[pl.ds(j0, tq), :]), v_ref[pl.ds(j0, tq), :], *carry)

    m, l, acc = lax.fori_loop(0, qi, body, (m, l, acc))
    o_ref[...] = (acc / l).astype(o_ref.dtype)


def _mla_prompt(qcat, kcat, v, batch, seq):
    tq = min(MLA_TILE, seq)
    nq = seq // tq
    return pl.pallas_call(
        _mla_prompt_kernel,
        grid=(batch, HEADS, nq),
        in_specs=[pl.BlockSpec((tq, QK_PAD), lambda b, h, i: (b * nq + i, h)),
                  pl.BlockSpec((seq, QK_PAD), lambda b, h, i: (b, h)),
                  pl.BlockSpec((seq, HEAD_DIM), lambda b, h, i: (b, h))],
        out_specs=pl.BlockSpec((tq, HEAD_DIM), lambda b, h, i: (b * nq + i, h)),
        out_shape=jax.ShapeDtypeStruct((batch * seq, HEADS * HEAD_DIM), BF16),
        compiler_params=_params(("parallel", "parallel", "arbitrary")),
        name="mla_prompt_attn",
    )(qcat, kcat, v)


def _moba_prompt_kernel(q_ref, k_ref, v_ref, mean_ref, tiles_ref, o_ref, sa_ref, sb_ref, *, group):
    i = pl.program_id(2)
    nb = mean_ref.shape[0]
    blk = MOBA_BLOCK
    pad = QK_PAD - HEAD_DIM
    qf = q_ref[...]
    qs = (qf * (MOBA_SCALE * LOG2E)).astype(BF16)
    d0 = pl.multiple_of(i * blk, blk)
    p0 = pl.multiple_of(jnp.maximum(i - 1, 0) * blk, blk)
    s_own = _dot_nt(qs, k_ref[pl.ds(d0, blk), :HEAD_DIM]) + tiles_ref[0]
    s_prev = _dot_nt(qs, k_ref[pl.ds(p0, blk), :HEAD_DIM]) + (tiles_ref[1] + jnp.where(i == 0, NEG, 0.0))

    qh, ql = _split(qf)
    mh, ml = _split(mean_ref[...])
    gate = _dot_nt(mh, qh) + _dot_nt(ml, qh) + _dot_nt(mh, ql)
    row = lax.broadcasted_iota(jnp.int32, (nb, blk), 0)
    gate = jnp.where(row < i, gate, NEG)
    rank = jnp.zeros((nb, blk), F32)
    for jp in range(nb):
        gj = gate[jp:jp + 1, :]
        rank = rank + jnp.where(row > jp, jnp.where(gj >= gate, 1.0, 0.0), jnp.where(gj > gate, 1.0, 0.0))
    sel = (rank < MOBA_TOPK) & (row < i)
    prev = jnp.where((row == i - 1) & sel, 0.0, NEG)
    far = jnp.where((row < i - 1) & sel, 0.0, NEG)
    fill = jnp.zeros((pad - nb, blk), F32)
    prev_t = jnp.concatenate([prev, fill], axis=0).T.astype(BF16)
    far_t = jnp.concatenate([far, fill], axis=0).T.astype(BF16)
    q_far = jnp.concatenate([qs, far_t], axis=-1)

    s_prev = s_prev + _dot_nt(prev_t, k_ref[pl.ds(p0, blk), HEAD_DIM:])
    m = jnp.maximum(jnp.max(s_own, axis=-1, keepdims=True), jnp.max(s_prev, axis=-1, keepdims=True))
    p_own = jnp.exp2(s_own - m)
    p_prev = jnp.exp2(s_prev - m)
    l = jnp.sum(p_own, axis=-1, keepdims=True) + jnp.sum(p_prev, axis=-1, keepdims=True)
    acc = _dot(p_own.astype(BF16), v_ref[pl.ds(d0, blk), :]) + _dot(p_prev.astype(BF16), v_ref[pl.ds(p0, blk), :])

    span = group * blk

    def logits(g):
        return _dot_nt(q_far, k_ref[pl.ds(pl.multiple_of(g * span, span), span), :])

    def consume(s_ref, g, carry):
        return _online_softmax_step(s_ref[...], v_ref[pl.ds(pl.multiple_of(g * span, span), span), :], *carry)

    n_spans = (jnp.maximum(i - 1, 0) + group - 1) // group
    n_pairs = (n_spans + 1) // 2
    last = nb // group - 1
    sa_ref[...] = logits(0)

    def body(pp, carry):
        sb_ref[...] = logits(jnp.minimum(2 * pp + 1, last))
        carry = consume(sa_ref, 2 * pp, carry)
        sa_ref[...] = logits(jnp.minimum(2 * pp + 2, last))
        return consume(sb_ref, jnp.minimum(2 * pp + 1, last), carry)

    m, l, acc = lax.fori_loop(0, n_pairs, body, (m, l, acc))
    o_ref[...] = (acc / l).astype(o_ref.dtype)


def _moba_prompt(qm, kext, vb, means, tiles, batch, seq):
    assert seq % MOBA_BLOCK == 0
    nb = seq // MOBA_BLOCK
    blk = MOBA_BLOCK
    group = math.gcd(nb, 2)
    return pl.pallas_call(
        functools.partial(_moba_prompt_kernel, group=group),
        scratch_shapes=[pltpu.VMEM((blk, group * blk), F32), pltpu.VMEM((blk, group * blk), F32)],
        grid=(batch, HEADS, nb),
        in_specs=[pl.BlockSpec((blk, HEAD_DIM), lambda b, h, i: (b * nb + i, h)),
                  pl.BlockSpec((seq, QK_PAD), lambda b, h, i: (b, h)),
                  pl.BlockSpec((seq, HEAD_DIM), lambda b, h, i: (b, h)),
                  pl.BlockSpec((None, nb, HEAD_DIM), lambda b, h, i: (b, 0, h)),
                  pl.BlockSpec((None, 2, blk, blk), lambda b, h, i: (h, 0, 0, 0))],
        out_specs=pl.BlockSpec((blk, HEAD_DIM), lambda b, h, i: (b * nb + i, h)),
        out_shape=jax.ShapeDtypeStruct((batch * seq, HEADS * HEAD_DIM), BF16),
        compiler_params=_params(("parallel", "parallel", "arbitrary")),
        name="moba_prompt_attn",
    )(qm, kext, vb, means, tiles)


SELECT_DEPTH = 3


def _decode_kernel(pt_ref, qabs_ref, qpe_ref, cnew_ref, kpenew_ref, wkt_ref, qm_ref, lat_hbm, kpet_hbm, k_hbm,
                   o_ref, idx_ref, cbuf, kbuf, cb16, s_scr, mbuf, sums, sem, msem, *, n_pages, tile, unroll):
    s_idx = pl.program_id(0)
    n_seq = pl.num_programs(0)
    slot = s_idx % 2
    past = n_pages * PAGE_SIZE
    dec = cnew_ref.shape[0]
    rows = qabs_ref.shape[0]
    n_tiles = past // tile
    chunk_pages = tile // PAGE_SIZE
    n_full = n_pages // PAGES_PER_BLOCK
    blocks_per_chunk = chunk_pages // PAGES_PER_BLOCK
    width = sums.shape[0]

    def copies(seq, sl, p):
        page = pt_ref[seq * n_pages + p]
        dst = pl.ds(pl.multiple_of(p * PAGE_SIZE, PAGE_SIZE), PAGE_SIZE)
        return (pltpu.make_async_copy(lat_hbm.at[page], cbuf.at[sl, dst], sem.at[0, sl]),
                pltpu.make_async_copy(kpet_hbm.at[page], kbuf.at[sl, :, dst], sem.at[1, sl]))

    def for_pages(seq, sl, fn):
        def body(i, _):
            for u in range(unroll):
                for cp in copies(seq, sl, i * unroll + u):
                    fn(cp)
            return 0
        lax.fori_loop(0, n_pages // unroll, body, 0)

    def chunk_copy(g, p):
        return pltpu.make_async_copy(k_hbm.at[pt_ref[g * chunk_pages + p]], mbuf.at[g % SELECT_DEPTH, p],
                                     msem.at[g % SELECT_DEPTH])

    def start_chunk(g):
        for p in range(chunk_pages):
            chunk_copy(g, p).start()

    @pl.when(s_idx == 0)
    def _():
        for_pages(s_idx, slot, lambda cp: cp.start())
        sums[...] = jnp.zeros(sums.shape, F32)
        for d in range(SELECT_DEPTH - 1):
            start_chunk(d)

    @pl.when(s_idx + 1 < n_seq)
    def _():
        for_pages(s_idx + 1, 1 - slot, lambda cp: cp.start())

    for_pages(s_idx, slot, lambda cp: cp.wait())

    qabs = qabs_ref[...]
    qpe = qpe_ref[...]
    wkt = wkt_ref[...]

    def block_sums(t):
        g = s_idx * n_tiles + t
        ahead = g + SELECT_DEPTH - 1

        @pl.when(ahead < n_seq * n_tiles)
        def _():
            start_chunk(ahead)

        for p in range(chunk_pages):
            chunk_copy(g, p).wait()
        for b in range(blocks_per_chunk):
            x = mbuf[g % SELECT_DEPTH, b * PAGES_PER_BLOCK:(b + 1) * PAGES_PER_BLOCK]
            n = t * blocks_per_chunk + b
            sums[pl.ds(pl.multiple_of(n * HEADS, HEADS), HEADS), :] = jnp.sum(x, axis=(0, 1))

    def scores(c_t, kpe_t):
        kn_t = _dot_nt(wkt, c_t)
        ssq = [jnp.sum(jnp.square(kn_t[h * HEAD_DIM:(h + 1) * HEAD_DIM, :]), axis=0, keepdims=True)
               for h in range(HEADS)]
        inv = lax.rsqrt(jnp.concatenate(ssq, axis=0) * (1.0 / HEAD_DIM) + EPS)
        inv = jnp.concatenate([inv] * dec, axis=0)
        return _dot_nt(qabs, c_t) * inv + _dot(qpe, kpe_t)

    def score_tile(t, _):
        t0 = pl.multiple_of(t * tile, tile)
        c_t = cbuf[slot, pl.ds(t0, tile), :].astype(BF16)
        cb16[pl.ds(t0, tile), :] = c_t
        s_scr[:, pl.ds(t0, tile)] = scores(c_t, kbuf[slot, :, pl.ds(t0, tile)].astype(BF16))
        block_sums(t)
        return 0

    lax.fori_loop(0, n_tiles, score_tile, 0)

    qh, ql = _split(qm_ref[...])
    mh, ml = _split(sums[...] * (1.0 / MOBA_BLOCK))
    gate = _dot_nt(qh, mh) + _dot_nt(qh, ml) + _dot_nt(ql, mh)
    col = lax.broadcasted_iota(jnp.int32, (rows, width), 1)
    row = lax.broadcasted_iota(jnp.int32, (rows, width), 0)
    valid = ((col & (HEADS - 1)) == (row & (HEADS - 1))) & (col < n_full * HEADS)
    gate = jnp.where(valid, gate, NEG)
    rank = jnp.zeros((rows, width), F32)
    for s in range(1, width // HEADS):
        other = pltpu.roll(gate, s * HEADS, axis=1)
        rank = rank + jnp.where(col >= s * HEADS, jnp.where(other >= gate, 1.0, 0.0), jnp.where(other > gate, 1.0, 0.0))
    blk = (col // HEADS).astype(F32)
    lane128 = lax.broadcasted_iota(jnp.int32, (rows, 128), 1)
    picks = jnp.zeros((rows, 128), F32)
    for k in range(MOBA_TOPK):
        pick = jnp.sum(jnp.where(valid & (rank == k), blk, 0.0), axis=-1, keepdims=True)
        picks = jnp.where(lane128 == k, pick, picks)
    idx_ref[...] = picks.astype(jnp.int32)

    c_new = jnp.concatenate([cnew_ref[...], jnp.zeros((PAGE_SIZE - dec, KV_LORA), F32)], axis=0).astype(BF16)
    s_new = scores(c_new, kpenew_ref[...].astype(BF16))
    qrow = lax.broadcasted_iota(jnp.int32, (rows, PAGE_SIZE), 0) // HEADS
    lane = lax.broadcasted_iota(jnp.int32, (rows, PAGE_SIZE), 1)
    s_scr[:, pl.ds(past, PAGE_SIZE)] = jnp.where(lane <= qrow, s_new, NEG)

    s_all = s_scr[...]
    m = jnp.max(s_all, axis=-1, keepdims=True)
    p_all = jnp.exp2(s_all - m)
    l = jnp.sum(p_all, axis=-1, keepdims=True)
    s_scr[...] = p_all

    def pv_tile(t, acc):
        t0 = pl.multiple_of(t * tile, tile)
        return acc + _dot(s_scr[:, pl.ds(t0, tile)].astype(BF16), cb16[pl.ds(t0, tile), :])

    acc = lax.fori_loop(0, n_tiles, pv_tile, _dot(s_scr[:, pl.ds(past, PAGE_SIZE)].astype(BF16), c_new))
    o_ref[...] = acc / l


def _largest_tile(total, unit, cap):
    n = total // unit
    best = 1
    for d in range(1, n + 1):
        if n % d == 0 and d * unit <= cap:
            best = d
    return best * unit


def _decode(page_table, qabs, qpe, c_new, kpe_new_t, wkt, qm, lat, kpe_cache_t, k4):
    db, n_pages = page_table.shape
    dec = c_new.shape[1]
    rows = dec * HEADS
    past = n_pages * PAGE_SIZE
    tile = _largest_tile(past, MOBA_BLOCK, 1024)
    chunk_pages = tile // PAGE_SIZE
    n_full = n_pages // PAGES_PER_BLOCK
    width = -(-n_full * HEADS // 128) * 128
    unroll = math.gcd(n_pages, 8)
    assert db * (past // tile) >= SELECT_DEPTH - 1
    per_seq = lambda n: pl.BlockSpec((None, rows, n), lambda s, pt: (s, 0, 0))
    return pl.pallas_call(
        functools.partial(_decode_kernel, n_pages=n_pages, tile=tile, unroll=unroll),
        grid_spec=pltpu.PrefetchScalarGridSpec(
            num_scalar_prefetch=1,
            grid=(db,),
            in_specs=[per_seq(KV_LORA), per_seq(ROPE_DIM),
                      pl.BlockSpec((None, dec, KV_LORA), lambda s, pt: (s, 0, 0)),
                      pl.BlockSpec((None, ROPE_DIM, PAGE_SIZE), lambda s, pt: (s, 0, 0)),
                      pl.BlockSpec(wkt.shape, lambda s, pt: (0, 0)),
                      per_seq(HEAD_DIM),
                      pl.BlockSpec(memory_space=pl.ANY),
                      pl.BlockSpec(memory_space=pl.ANY),
                      pl.BlockSpec(memory_space=pl.ANY)],
            out_specs=[per_seq(KV_LORA), per_seq(128)],
            scratch_shapes=[pltpu.VMEM((2, past, KV_LORA), F32),
                            pltpu.VMEM((2, ROPE_DIM, past), F32),
                            pltpu.VMEM((past, KV_LORA), BF16),
                            pltpu.VMEM((rows, past + PAGE_SIZE), F32),
                            pltpu.VMEM((SELECT_DEPTH, chunk_pages, PAGE_SIZE, HEADS, HEAD_DIM), F32),
                            pltpu.VMEM((width, HEAD_DIM), F32),
                            pltpu.SemaphoreType.DMA((2, 2)),
                            pltpu.SemaphoreType.DMA((SELECT_DEPTH,))]),
        out_shape=[jax.ShapeDtypeStruct((db, rows, KV_LORA), F32), jax.ShapeDtypeStruct((db, rows, 128), jnp.int32)],
        compiler_params=_params(("arbitrary",)),
        name="decode_mla_attn_moba_select",
    )(page_table.reshape(-1), qabs, qpe, c_new, kpe_new_t, wkt, qm, lat, kpe_cache_t, k4)


def _mla_out_kernel(x_ref, wv_ref, o_ref):
    for h in range(HEADS):
        hi, lo = _split(x_ref[:, h * KV_LORA:(h + 1) * KV_LORA])
        wv = wv_ref[:, h * HEAD_DIM:(h + 1) * HEAD_DIM]
        o_ref[:, h * HEAD_DIM:(h + 1) * HEAD_DIM] = (_dot(hi, wv) + _dot(lo, wv)).astype(o_ref.dtype)


def _mla_out(x, wv):
    t = x.shape[0]
    return pl.pallas_call(
        _mla_out_kernel,
        grid=(1,),
        in_specs=[pl.BlockSpec(x.shape, lambda i: (0, 0)), pl.BlockSpec(wv.shape, lambda i: (0, 0))],
        out_specs=pl.BlockSpec((t, HEADS * HEAD_DIM), lambda i: (0, 0)),
        out_shape=jax.ShapeDtypeStruct((t, HEADS * HEAD_DIM), BF16),
        compiler_params=_params(("arbitrary",)),
        name="mla_sample_out",
    )(x, wv)


def _moba_sample_kernel(pt_ref, idx_ref, q_ref, kn_ref, vn_ref, last_ref, new_ref, k_hbm, v_hbm, o_ref,
                        kbuf, vbuf, sem, *, n_pages, dec):
    s_idx = pl.program_id(0)
    h = pl.program_id(1)
    g = s_idx * HEADS + h
    n_steps = pl.num_programs(0) * HEADS
    slot = g % 2
    n_full = n_pages // PAGES_PER_BLOCK
    n_sel = MOBA_TOPK * dec

    def copies(step, sl, q, k, j):
        seq = step // HEADS
        head = step % HEADS
        blk = idx_ref[step * n_sel + q * MOBA_TOPK + k]
        page = pt_ref[seq * n_pages + blk * PAGES_PER_BLOCK + j]
        dst = pl.ds((k * PAGES_PER_BLOCK + j) * PAGE_SIZE, PAGE_SIZE)
        return (pltpu.make_async_copy(k_hbm.at[page, :, head, :], kbuf.at[sl, q, dst], sem.at[0, sl]),
                pltpu.make_async_copy(v_hbm.at[page, :, head, :], vbuf.at[sl, q, dst], sem.at[1, sl]))

    def for_all(step, sl, fn):
        for q in range(dec):
            for k in range(MOBA_TOPK):
                for j in range(PAGES_PER_BLOCK):
                    for cp in copies(step, sl, q, k, j):
                        fn(cp)

    @pl.when(g == 0)
    def _():
        for_all(g, slot, lambda cp: cp.start())

    @pl.when(g + 1 < n_steps)
    def _():
        for_all(g + 1, 1 - slot, lambda cp: cp.start())

    for_all(g, slot, lambda cp: cp.wait())

    seg = MOBA_TOPK * MOBA_BLOCK
    total = dec * seg
    q8 = (q_ref[...] * (MOBA_SCALE * LOG2E)).astype(BF16)
    kn = kn_ref[...].astype(BF16)
    vn = vn_ref[...].astype(BF16)
    k_all = kbuf[slot].reshape(total, HEAD_DIM).astype(BF16)
    v_all = vbuf[slot].reshape(total, HEAD_DIM).astype(BF16)
    last = last_ref[...]
    bias = [jnp.where(idx_ref[g * n_sel + i] == n_full - 1, last, 0.0) for i in range(n_sel)]
    row = lax.broadcasted_iota(jnp.int32, (8, total), 0)
    col = lax.broadcasted_iota(jnp.int32, (8, total), 1)
    own = (col >= row * seg) & (col < (row + 1) * seg)
    s = jnp.where(own, _dot_nt(q8, k_all) + jnp.concatenate(bias, axis=-1), NEG)
    sn = _dot_nt(q8, kn) + new_ref[:, :8]
    m = jnp.maximum(jnp.max(s, axis=-1, keepdims=True), jnp.max(sn, axis=-1, keepdims=True))
    p = jnp.exp2(s - m)
    pn = jnp.exp2(sn - m)
    l = jnp.sum(p, axis=-1, keepdims=True) + jnp.sum(pn, axis=-1, keepdims=True)
    o = (_dot(p.astype(BF16), v_all) + _dot(pn.astype(BF16), vn)) / l
    o_ref[...] = o[:dec, :]


def _moba_sample(page_table, idx, q8, kn8, vn8, bias_last, bias_new, k4, v4, dec):
    db, n_pages = page_table.shape
    per = lambda: pl.BlockSpec((None, None, 8, HEAD_DIM), lambda s, h, pt, ix: (s, h, 0, 0))
    return pl.pallas_call(
        functools.partial(_moba_sample_kernel, n_pages=n_pages, dec=dec),
        grid_spec=pltpu.PrefetchScalarGridSpec(
            num_scalar_prefetch=2,
            grid=(db, HEADS),
            in_specs=[per(), per(), per(),
                      pl.BlockSpec((None, 8, MOBA_BLOCK), lambda s, h, pt, ix: (h, 0, 0)),
                      pl.BlockSpec((None, 8, 128), lambda s, h, pt, ix: (h, 0, 0)),
                      pl.BlockSpec(memory_space=pl.ANY),
                      pl.BlockSpec(memory_space=pl.ANY)],
            out_specs=pl.BlockSpec((None, dec, HEAD_DIM), lambda s, h, pt, ix: (s, 0, h)),
            scratch_shapes=[pltpu.VMEM((2, dec, MOBA_TOPK * MOBA_BLOCK, HEAD_DIM), F32),
                            pltpu.VMEM((2, dec, MOBA_TOPK * MOBA_BLOCK, HEAD_DIM), F32),
                            pltpu.SemaphoreType.DMA((2, 2))]),
        out_shape=jax.ShapeDtypeStruct((db, dec, HEADS * HEAD_DIM), F32),
        compiler_params=_params(("arbitrary", "arbitrary")),
        name="moba_sample_attn",
    )(page_table.reshape(-1), idx, q8, kn8, vn8, bias_last, bias_new, k4, v4)


def _merge_ffn_kernel(x_ref, oa_ref, ob_ref, ga_ref, wg_ref, wo_ref, gf_ref, wgu_ref, wd_ref, y_ref):
    x = x_ref[...]
    xn = _rms(x, ga_ref[...]).astype(BF16)
    g = _dot(xn, wg_ref[...])
    mix = jax.nn.sigmoid(g[:, :D_MODEL]) * oa_ref[...].astype(F32) + jax.nn.sigmoid(g[:, D_MODEL:]) * ob_ref[...].astype(F32)
    hres = x + _dot(mix.astype(BF16), wo_ref[...])
    gu = _dot(_rms(hres, gf_ref[...]).astype(BF16), wgu_ref[...])
    d_ff = wd_ref.shape[0]
    act = jax.nn.silu(gu[:, :d_ff]) * gu[:, d_ff:]
    y_ref[...] = hres + _dot(act.astype(BF16), wd_ref[...])


def _merge_ffn(x, o_a, o_b, w):
    t = x.shape[0]
    tm = ROW_TILE
    row = lambda: pl.BlockSpec((tm, D_MODEL), lambda i: (i, 0))
    consts = [w["norm_attn"], w["w_g"], w["w_o"], w["norm_ffn"], w["w_gate_up"], w["w_down"]]
    return pl.pallas_call(
        _merge_ffn_kernel,
        grid=(t // tm,),
        in_specs=[row(), row(), row()] + [_const_spec(a.shape) for a in consts],
        out_specs=row(),
        out_shape=jax.ShapeDtypeStruct((t, D_MODEL), F32),
        compiler_params=_params(("parallel",)),
        name="merge_ffn",
    )(x, o_a, o_b, *consts)


def _rope_tables(pos):
    half = ROPE_DIM // 2
    freqs = ROPE_THETA ** (-jnp.arange(half, dtype=F32) / half)
    ang = pos.astype(F32)[:, None] * freqs
    cos, sin = jnp.cos(ang), jnp.sin(ang)
    return jnp.concatenate([cos, cos], axis=-1), jnp.concatenate([-sin, sin], axis=-1)


def _layer_weights(l, norm_attn, w_in, q_a_norm, w_q_b, kv_a_norm, w_kv_b, mla_qn_gain, mla_qr_gain, mla_kn_gain,
                   mla_kr_gain, moba_q_gain, moba_k_gain, w_o, norm_ffn, w_gate_up, w_down):
    width = HEADS * HEAD_DIM
    a_end = Q_LORA + KV_LORA + ROPE_DIM
    row = lambda a: a[l].reshape(1, -1)
    win = w_in[l].astype(BF16)
    wqb = w_q_b[l].reshape(Q_LORA, HEADS, HEAD_DIM + ROPE_DIM)
    wkv = w_kv_b[l].reshape(KV_LORA, HEADS, 2 * HEAD_DIM)
    return {
        "norm_attn": row(norm_attn), "q_a_norm": row(q_a_norm), "kv_a_norm": row(kv_a_norm),
        "qn_gain": row(mla_qn_gain), "qr_gain": row(mla_qr_gain), "kn_gain": row(mla_kn_gain),
        "kr_gain": row(mla_kr_gain), "moba_q_gain": row(moba_q_gain), "moba_k_gain": row(moba_k_gain),
        "norm_ffn": row(norm_ffn),
        "w_a": win[:, :a_end], "w_m": win[:, a_end:a_end + 3 * width], "w_g": win[:, a_end + 3 * width:],
        "w_q_b": jnp.concatenate([wqb[:, :, :HEAD_DIM].reshape(Q_LORA, -1),
                                  wqb[:, :, HEAD_DIM:].reshape(Q_LORA, -1)], axis=-1).astype(BF16),
        "w_kv": jnp.concatenate([wkv[:, :, :HEAD_DIM].reshape(KV_LORA, -1),
                                 wkv[:, :, HEAD_DIM:].reshape(KV_LORA, -1)], axis=-1).astype(BF16),
        "w_o": w_o[l].astype(BF16), "w_gate_up": w_gate_up[l].astype(BF16), "w_down": w_down[l].astype(BF16),
    }


def kernel(x_prompt, x_sample, cache_mla_latent, cache_mla_kpe, cache_moba_k, cache_moba_v, page_table, norm_attn, w_in, q_a_norm, w_q_b, kv_a_norm, w_kv_b, mla_qn_gain, mla_qr_gain, mla_kn_gain, mla_kr_gain, moba_q_gain, moba_k_gain, t5_table, w_o, norm_ffn, w_gate_up, w_down):
    batch, seq, _ = x_prompt.shape
    db, dec, _ = x_sample.shape
    depth = w_in.shape[0]
    n_pool = cache_mla_latent.shape[1]
    n_pages = page_table.shape[1]
    past = n_pages * PAGE_SIZE
    width = HEADS * HEAD_DIM
    assert seq % ROW_TILE == 0 and (db * dec) % ROW_TILE == 0 and ROW_TILE % dec == 0
    assert past % MOBA_BLOCK == 0 and past // MOBA_BLOCK >= MOBA_TOPK and dec <= 8

    cos_p, sin_p = _rope_tables(jnp.arange(seq))
    cos_s, sin_s = _rope_tables(past + jnp.arange(ROW_TILE) % dec)
    tiles, bias_last, bias_new = _t5_tables(t5_table, 8)

    yp = x_prompt.reshape(batch * seq, D_MODEL)
    ys = x_sample.reshape(db * dec, D_MODEL)
    outs = [[] for _ in range(8)]
    for l in range(depth):
        w = _layer_weights(l, norm_attn, w_in, q_a_norm, w_q_b, kv_a_norm, w_kv_b, mla_qn_gain, mla_qr_gain,
                           mla_kn_gain, mla_kr_gain, moba_q_gain, moba_k_gain, w_o, norm_ffn, w_gate_up, w_down)
        c, kpe, qcat, kcat, v = _mla_proj(yp, w, cos_p, sin_p, sample=False)
        qm, km, vm, kext, vb, means = _moba_proj(yp, w, blocks_per_seq=seq // MOBA_BLOCK)
        o_a = _mla_prompt(qcat, kcat, v, batch, seq)
        o_b = _moba_prompt(qm, kext, vb, means.reshape(batch, seq // MOBA_BLOCK, width), tiles, batch, seq)
        outs[0].append(c.reshape(batch, seq, KV_LORA))
        outs[1].append(kpe.reshape(batch, seq, ROPE_DIM))
        outs[2].append(km.reshape(batch, seq, HEADS, HEAD_DIM))
        outs[3].append(vm.reshape(batch, seq, HEADS, HEAD_DIM))
        yp = _merge_ffn(yp, o_a, o_b, w)
        c, kpe, qabs, qpe = _mla_proj(ys, w, cos_s, sin_s, sample=True)
        qm, km, vm = _moba_proj(ys, w)
        kpe_new_t = jnp.pad(jnp.swapaxes(kpe.reshape(db, dec, ROPE_DIM), 1, 2), ((0, 0), (0, 0), (0, PAGE_SIZE - dec)))
        k4 = cache_moba_k[l]
        v4 = cache_moba_v[l]
        lat, idx = _decode(page_table, qabs.reshape(db, dec * HEADS, KV_LORA), qpe.reshape(db, dec * HEADS, ROPE_DIM),
                           c.reshape(db, dec, KV_LORA), kpe_new_t, w["w_kv"][:, :width].T,
                           qm.reshape(db, dec * HEADS, HEAD_DIM),
                           cache_mla_latent[l], jnp.swapaxes(cache_mla_kpe[l], 1, 2), k4)
        o_a = _mla_out(lat.reshape(db * dec, HEADS * KV_LORA), w["w_kv"][:, width:])
        idx = jnp.transpose(idx[:, :, :MOBA_TOPK].reshape(db, dec, HEADS, MOBA_TOPK), (0, 2, 1, 3)).reshape(-1)
        to_heads = lambda a: jnp.pad(jnp.transpose(a.reshape(db, dec, HEADS, HEAD_DIM), (0, 2, 1, 3)),
                                     ((0, 0), (0, 0), (0, 8 - dec), (0, 0)))
        o_b = _moba_sample(page_table, idx, to_heads(qm), to_heads(km), to_heads(vm), bias_last, bias_new, k4, v4, dec)
        outs[4].append(c.reshape(db, dec, KV_LORA))
        outs[5].append(kpe.reshape(db, dec, ROPE_DIM))
        outs[6].append(km.reshape(db, dec, HEADS, HEAD_DIM))
        outs[7].append(vm.reshape(db, dec, HEADS, HEAD_DIM))
        ys = _merge_ffn(ys, o_a, o_b.reshape(db * dec, width), w)
    return (yp.reshape(batch, seq, D_MODEL), ys.reshape(db, dec, D_MODEL)) + tuple(jnp.stack(o, axis=0) for o in outs)
```

```python
import functools
import math

import numpy as np
import jax
import jax.numpy as jnp
from jax import lax
from jax.experimental import pallas as pl
from jax.experimental.pallas import tpu as pltpu

F32 = jnp.float32
BF16 = jnp.bfloat16

D_MODEL = 1024
HEADS = 8
HEAD_DIM = 128
ROPE_DIM = 64
Q_LORA = 256
KV_LORA = 256
QK_PAD = 256
ROPE_THETA = 10000.0
MLA_SCALE = (HEAD_DIM + ROPE_DIM) ** -0.5
MOBA_BLOCK = 256
MOBA_TOPK = 3
MOBA_SCALE = HEAD_DIM ** -0.5
PAGE_SIZE = 128
PAGES_PER_BLOCK = MOBA_BLOCK // PAGE_SIZE
T5_BUCKETS = 32
T5_MAX_DIST = 128
EPS = 1e-6
NEG = -1e30
LOG2E = math.log2(math.e)
ROW_TILE = 256
MLA_TILE = 512
VMEM_LIMIT = 56 * 1024 * 1024

_NT = (((1,), (1,)), ((), ()))


def _dot(a, b):
    return jnp.dot(a, b, preferred_element_type=F32)


def _dot_nt(a, b):
    return lax.dot_general(a, b, _NT, preferred_element_type=F32)


def _split(x):
    hi = x.astype(BF16)
    lo = (x - hi.astype(F32)).astype(BF16)
    return hi, lo


def _rms(x, g):
    ms = jnp.mean(x * x, axis=-1, keepdims=True)
    return x * lax.rsqrt(ms + EPS) * g


def _rope(x, cosf, sinf):
    half = ROPE_DIM // 2
    swapped = jnp.concatenate([x[:, half:], x[:, :half]], axis=-1)
    return x * cosf + swapped * sinf


def _const_spec(shape):
    n = len(shape)
    return pl.BlockSpec(shape, lambda *_: (0,) * n, pipeline_mode=pl.Buffered(1))


def _params(sem):
    return pltpu.CompilerParams(dimension_semantics=sem, vmem_limit_bytes=VMEM_LIMIT)


def _t5_bucket_upper_bounds(max_dist):
    d = np.arange(max_dist + 1)
    max_exact = T5_BUCKETS // 2
    dd = np.maximum(d, max_exact).astype(np.float32)
    large = max_exact + (np.log(dd / np.float32(max_exact)) / np.float32(math.log(T5_MAX_DIST / max_exact))
                         * np.float32(T5_BUCKETS - max_exact)).astype(np.int32)
    bucket = np.where(d < max_exact, d, np.minimum(large, T5_BUCKETS - 1))
    assert bucket[-1] == T5_BUCKETS - 1
    return [int(d[bucket == b].max()) if np.any(bucket == b) else None for b in range(T5_BUCKETS)]


def _t5_bias(dist, h, t5_ref, bounds):
    res = jnp.full(dist.shape, t5_ref[T5_BUCKETS - 1, h], F32)
    for b in range(T5_BUCKETS - 2, -1, -1):
        if bounds[b] is not None:
            res = jnp.where(dist <= bounds[b], t5_ref[b, h], res)
    return res


def _t5_kernel(t5_ref, tiles_ref, last_ref, new_ref, *, bounds, dec_seq):
    h = pl.program_id(0)
    far = t5_ref[T5_BUCKETS - 1, h]
    rel = lambda dist: (_t5_bias(dist, h, t5_ref, bounds) - far) * LOG2E
    r = lax.broadcasted_iota(jnp.int32, (MOBA_BLOCK, MOBA_BLOCK), 0)
    c = lax.broadcasted_iota(jnp.int32, (MOBA_BLOCK, MOBA_BLOCK), 1)
    tiles_ref[0] = jnp.where(c <= r, rel(r - c), NEG)
    tiles_ref[1] = rel(MOBA_BLOCK + r - c)
    q = lax.broadcasted_iota(jnp.int32, (dec_seq, MOBA_BLOCK), 0)
    j = lax.broadcasted_iota(jnp.int32, (dec_seq, MOBA_BLOCK), 1)
    last_ref[...] = rel(MOBA_BLOCK + q - j)
    qn = lax.broadcasted_iota(jnp.int32, (dec_seq, 128), 0)
    jn = lax.broadcasted_iota(jnp.int32, (dec_seq, 128), 1)
    new_ref[...] = jnp.where(jn <= qn, rel(qn - jn), NEG)


def _t5_tables(t5_table, dec_seq):
    bounds = _t5_bucket_upper_bounds(2 * MOBA_BLOCK + dec_seq)
    return pl.pallas_call(
        functools.partial(_t5_kernel, bounds=bounds, dec_seq=dec_seq),
        grid=(HEADS,),
        in_specs=[pl.BlockSpec(memory_space=pltpu.SMEM)],
        out_specs=[pl.BlockSpec((None, 2, MOBA_BLOCK, MOBA_BLOCK), lambda h: (h, 0, 0, 0)),
                   pl.BlockSpec((None, dec_seq, MOBA_BLOCK), lambda h: (h, 0, 0)),
                   pl.BlockSpec((None, dec_seq, 128), lambda h: (h, 0, 0))],
        out_shape=[jax.ShapeDtypeStruct((HEADS, 2, MOBA_BLOCK, MOBA_BLOCK), F32),
                   jax.ShapeDtypeStruct((HEADS, dec_seq, MOBA_BLOCK), F32),
                   jax.ShapeDtypeStruct((HEADS, dec_seq, 128), F32)],
        compiler_params=_params(("arbitrary",)),
        name="t5_bias",
    )(t5_table)


def _mla_proj_kernel(x_ref, g_ref, wa_ref, qan_ref, wqb_ref, kvn_ref, qng_ref, qrg_ref, krg_ref, kng_ref,
                     cos_ref, sin_ref, wkv_ref, c_ref, kpe_ref, *outs, sample):
    tm = x_ref.shape[0]
    xn = _rms(x_ref[...], g_ref[...]).astype(BF16)
    pa = _dot(xn, wa_ref[...])
    cosf, sinf = cos_ref[...], sin_ref[...]
    c = _rms(pa[:, Q_LORA:Q_LORA + KV_LORA], kvn_ref[...])
    kpe = _rope(_rms(pa[:, Q_LORA + KV_LORA:], krg_ref[...]), cosf, sinf)
    c_ref[...] = c
    kpe_ref[...] = kpe
    q = _dot(_rms(pa[:, :Q_LORA], qan_ref[...]).astype(BF16), wqb_ref[...])
    nope_w = HEADS * HEAD_DIM
    pad = jnp.zeros((tm, QK_PAD - HEAD_DIM - ROPE_DIM), F32)
    if sample:
        qabs_ref, qpe_ref = outs
        qpes = []
    else:
        qcat_ref, kcat_ref, v_ref = outs
        kv = _dot(c.astype(BF16), wkv_ref[...])
        v_ref[...] = kv[:, nope_w:].astype(BF16)
    for h in range(HEADS):
        qn = _rms(q[:, h * HEAD_DIM:(h + 1) * HEAD_DIM], qng_ref[...]) * (MLA_SCALE * LOG2E)
        qp = _rope(_rms(q[:, nope_w + h * ROPE_DIM:nope_w + (h + 1) * ROPE_DIM], qrg_ref[...]), cosf, sinf) * (MLA_SCALE * LOG2E)
        if sample:
            hi, lo = _split(qn * kng_ref[...])
            wk = wkv_ref[:, h * HEAD_DIM:(h + 1) * HEAD_DIM]
            qabs_ref[:, h * KV_LORA:(h + 1) * KV_LORA] = (_dot_nt(hi, wk) + _dot_nt(lo, wk)).astype(BF16)
            qpes.append(qp)
        else:
            kn = _rms(kv[:, h * HEAD_DIM:(h + 1) * HEAD_DIM], kng_ref[...])
            qcat_ref[:, h * QK_PAD:(h + 1) * QK_PAD] = jnp.concatenate([qn, qp, pad], axis=-1).astype(BF16)
            kcat_ref[:, h * QK_PAD:(h + 1) * QK_PAD] = jnp.concatenate([kn, kpe, pad], axis=-1).astype(BF16)
    if sample:
        qpe_ref[...] = jnp.concatenate(qpes, axis=-1).astype(BF16)


def _mla_proj(x, w, cosf, sinf, *, sample):
    t = x.shape[0]
    tm = ROW_TILE
    n_pos = cosf.shape[0] // tm
    row = lambda width: pl.BlockSpec((tm, width), lambda i: (i, 0))
    pos = pl.BlockSpec((tm, ROPE_DIM), lambda i: (i % n_pos, 0))
    consts = [w["norm_attn"], w["w_a"], w["q_a_norm"], w["w_q_b"], w["kv_a_norm"], w["qn_gain"], w["qr_gain"],
              w["kr_gain"], w["kn_gain"]]
    if sample:
        widths, dtypes = [KV_LORA, ROPE_DIM, HEADS * KV_LORA, HEADS * ROPE_DIM], [F32, F32, BF16, BF16]
    else:
        widths, dtypes = [KV_LORA, ROPE_DIM, HEADS * QK_PAD, HEADS * QK_PAD, HEADS * HEAD_DIM], [F32, F32, BF16, BF16, BF16]
    return pl.pallas_call(
        functools.partial(_mla_proj_kernel, sample=sample),
        grid=(t // tm,),
        in_specs=[row(D_MODEL)] + [_const_spec(a.shape) for a in consts] + [pos, pos, _const_spec(w["w_kv"].shape)],
        out_specs=[row(n) for n in widths],
        out_shape=[jax.ShapeDtypeStruct((t, n), d) for n, d in zip(widths, dtypes)],
        compiler_params=_params(("parallel",)),
        name="mla_proj_sample" if sample else "mla_proj_prompt",
    )(x, *consts, cosf, sinf, w["w_kv"])


def _moba_proj_kernel(x_ref, g_ref, wm_ref, qg_ref, kg_ref, q_ref, k_ref, v_ref, *prompt_refs, blocks_per_seq):
    tm = x_ref.shape[0]
    xn = _rms(x_ref[...], g_ref[...]).astype(BF16)
    pm = _dot(xn, wm_ref[...])
    w = HEADS * HEAD_DIM
    v = pm[:, 2 * w:]
    v_ref[...] = v
    if prompt_refs:
        kext_ref, vb_ref, mean_ref = prompt_refs
        vb_ref[...] = v.astype(BF16)
        lane = lax.broadcasted_iota(jnp.int32, (tm, QK_PAD - HEAD_DIM), 1)
        onehot = jnp.where(lane == pl.program_id(0) % blocks_per_seq, 1.0, 0.0).astype(BF16)
    for h in range(HEADS):
        sl = slice(h * HEAD_DIM, (h + 1) * HEAD_DIM)
        q_ref[:, sl] = _rms(pm[:, h * HEAD_DIM:(h + 1) * HEAD_DIM], qg_ref[...])
        k = _rms(pm[:, w + h * HEAD_DIM:w + (h + 1) * HEAD_DIM], kg_ref[...])
        k_ref[:, sl] = k
        if prompt_refs:
            kext_ref[:, h * QK_PAD:(h + 1) * QK_PAD] = jnp.concatenate([k.astype(BF16), onehot], axis=-1)
            mean_ref[:, sl] = jnp.mean(k, axis=0, keepdims=True)


def _moba_proj(x, w, *, blocks_per_seq=None):
    t = x.shape[0]
    tm = ROW_TILE
    width = HEADS * HEAD_DIM
    prompt = blocks_per_seq is not None
    row = lambda n=width: pl.BlockSpec((tm, n), lambda i: (i, 0))
    consts = [w["norm_attn"], w["w_m"], w["moba_q_gain"], w["moba_k_gain"]]
    out_specs = [row() for _ in range(3)]
    out_shape = [jax.ShapeDtypeStruct((t, width), F32) for _ in range(3)]
    if prompt:
        assert tm == MOBA_BLOCK and blocks_per_seq <= QK_PAD - HEAD_DIM
        out_specs += [row(HEADS * QK_PAD), row(), pl.BlockSpec((None, 1, width), lambda i: (i, 0, 0))]
        out_shape += [jax.ShapeDtypeStruct((t, HEADS * QK_PAD), BF16), jax.ShapeDtypeStruct((t, width), BF16),
                      jax.ShapeDtypeStruct((t // tm, 1, width), F32)]
    return pl.pallas_call(
        functools.partial(_moba_proj_kernel, blocks_per_seq=blocks_per_seq),
        grid=(t // tm,),
        in_specs=[pl.BlockSpec((tm, D_MODEL), lambda i: (i, 0))] + [_const_spec(a.shape) for a in consts],
        out_specs=out_specs,
        out_shape=out_shape,
        compiler_params=_params(("parallel",)),
        name="moba_proj_prompt" if prompt else "moba_proj_sample",
    )(x, *consts)


def _online_softmax_step(s, v, m, l, acc):
    m_new = jnp.maximum(m, jnp.max(s, axis=-1, keepdims=True))
    a = jnp.exp2(m - m_new)
    p = jnp.exp2(s - m_new)
    return m_new, a * l + jnp.sum(p, axis=-1, keepdims=True), a * acc + _dot(p.astype(BF16), v)


def _mla_prompt_kernel(q_ref, k_ref, v_ref, o_ref):
    qi = pl.program_id(2)
    tq = q_ref.shape[0]
    q = q_ref[...]
    d0 = pl.multiple_of(qi * tq, tq)
    s = _dot_nt(q, k_ref[pl.ds(d0, tq), :])
    r = lax.broadcasted_iota(jnp.int32, (tq, tq), 0)
    c = lax.broadcasted_iota(jnp.int32, (tq, tq), 1)
    s = jnp.where(c <= r, s, NEG)
    m = jnp.max(s, axis=-1, keepdims=True)
    p = jnp.exp2(s - m)
    l = jnp.sum(p, axis=-1, keepdims=True)
    acc = _dot(p.astype(BF16), v_ref[pl.ds(d0, tq), :])

    def body(j, carry):
        j0 = pl.multiple_of(j * tq, tq)
        return _online_softmax_step(_dot_nt(q, k_ref[pl.ds(j0, tq), :]), v_ref[pl.ds(j0, tq), :], *carry)

    m, l, acc = lax.fori_loop(0, qi, body, (m, l, acc))
    o_ref[...] = (acc / l).astype(o_ref.dtype)


def _mla_prompt(qcat, kcat, v, batch, seq):
    tq = min(MLA_TILE, seq)
    nq = seq // tq
    return pl.pallas_call(
        _mla_prompt_kernel,
        grid=(batch, HEADS, nq),
        in_specs=[pl.BlockSpec((tq, QK_PAD), lambda b, h, i: (b * nq + i, h)),
                  pl.BlockSpec((seq, QK_PAD), lambda b, h, i: (b, h)),
                  pl.BlockSpec((seq, HEAD_DIM), lambda b, h, i: (b, h))],
        out_specs=pl.BlockSpec((tq, HEAD_DIM), lambda b, h, i: (b * nq + i, h)),
        out_shape=jax.ShapeDtypeStruct((batch * seq, HEADS * HEAD_DIM), BF16),
        compiler_params=_params(("parallel", "parallel", "arbitrary")),
        name="mla_prompt_attn",
    )(qcat, kcat, v)


def _moba_prompt_kernel(q_ref, k_ref, v_ref, mean_ref, tiles_ref, o_ref, sa_ref, sb_ref, *, group):
    i = pl.program_id(2)
    nb = mean_ref.shape[0]
    blk = MOBA_BLOCK
    pad = QK_PAD - HEAD_DIM
    qf = q_ref[...]
    qs = (qf * (MOBA_SCALE * LOG2E)).astype(BF16)
    d0 = pl.multiple_of(i * blk, blk)
    p0 = pl.multiple_of(jnp.maximum(i - 1, 0) * blk, blk)
    s_own = _dot_nt(qs, k_ref[pl.ds(d0, blk), :HEAD_DIM]) + tiles_ref[0]
    s_prev = _dot_nt(qs, k_ref[pl.ds(p0, blk), :HEAD_DIM]) + (tiles_ref[1] + jnp.where(i == 0, NEG, 0.0))

    qh, ql = _split(qf)
    mh, ml = _split(mean_ref[...])
    gate = _dot_nt(mh, qh) + _dot_nt(ml, qh) + _dot_nt(mh, ql)
    row = lax.broadcasted_iota(jnp.int32, (nb, blk), 0)
    gate = jnp.where(row < i, gate, NEG)
    rank = jnp.zeros((nb, blk), F32)
    for jp in range(nb):
        gj = gate[jp:jp + 1, :]
        rank = rank + jnp.where(row > jp, jnp.where(gj >= gate, 1.0, 0.0), jnp.where(gj > gate, 1.0, 0.0))
    sel = (rank < MOBA_TOPK) & (row < i)
    prev = jnp.where((row == i - 1) & sel, 0.0, NEG)
    far = jnp.where((row < i - 1) & sel, 0.0, NEG)
    fill = jnp.zeros((pad - nb, blk), F32)
    prev_t = jnp.concatenate([prev, fill], axis=0).T.astype(BF16)
    far_t = jnp.concatenate([far, fill], axis=0).T.astype(BF16)
    q_far = jnp.concatenate([qs, far_t], axis=-1)

    s_prev = s_prev + _dot_nt(prev_t, k_ref[pl.ds(p0, blk), HEAD_DIM:])
    m = jnp.maximum(jnp.max(s_own, axis=-1, keepdims=True), jnp.max(s_prev, axis=-1, keepdims=True))
    p_own = jnp.exp2(s_own - m)
    p_prev = jnp.exp2(s_prev - m)
    l = jnp.sum(p_own, axis=-1, keepdims=True) + jnp.sum(p_prev, axis=-1, keepdims=True)
    acc = _dot(p_own.astype(BF16), v_ref[pl.ds(d0, blk), :]) + _dot(p_prev.astype(BF16), v_ref[pl.ds(p0, blk), :])

    span = group * blk

    def logits(g):
        return _dot_nt(q_far, k_ref[pl.ds(pl.multiple_of(g * span, span), span), :])

    def consume(s_ref, g, carry):
        return _online_softmax_step(s_ref[...], v_ref[pl.ds(pl.multiple_of(g * span, span), span), :], *carry)

    n_spans = (jnp.maximum(i - 1, 0) + group - 1) // group
    n_pairs = (n_spans + 1) // 2
    last = nb // group - 1
    sa_ref[...] = logits(0)

    def body(pp, carry):
        sb_ref[...] = logits(jnp.minimum(2 * pp + 1, last))
        carry = consume(sa_ref, 2 * pp, carry)
        sa_ref[...] = logits(jnp.minimum(2 * pp + 2, last))
        return consume(sb_ref, jnp.minimum(2 * pp + 1, last), carry)

    m, l, acc = lax.fori_loop(0, n_pairs, body, (m, l, acc))
    o_ref[...] = (acc / l).astype(o_ref.dtype)


def _moba_prompt(qm, kext, vb, means, tiles, batch, seq):
    assert seq % MOBA_BLOCK == 0
    nb = seq // MOBA_BLOCK
    blk = MOBA_BLOCK
    group = math.gcd(nb, 2)
    return pl.pallas_call(
        functools.partial(_moba_prompt_kernel, group=group),
        scratch_shapes=[pltpu.VMEM((blk, group * blk), F32), pltpu.VMEM((blk, group * blk), F32)],
        grid=(batch, HEADS, nb),
        in_specs=[pl.BlockSpec((blk, HEAD_DIM), lambda b, h, i: (b * nb + i, h)),
                  pl.BlockSpec((seq, QK_PAD), lambda b, h, i: (b, h)),
                  pl.BlockSpec((seq, HEAD_DIM), lambda b, h, i: (b, h)),
                  pl.BlockSpec((None, nb, HEAD_DIM), lambda b, h, i: (b, 0, h)),
                  pl.BlockSpec((None, 2, blk, blk), lambda b, h, i: (h, 0, 0, 0))],
        out_specs=pl.BlockSpec((blk, HEAD_DIM), lambda b, h, i: (b * nb + i, h)),
        out_shape=jax.ShapeDtypeStruct((batch * seq, HEADS * HEAD_DIM), BF16),
        compiler_params=_params(("parallel", "parallel", "arbitrary")),
        name="moba_prompt_attn",
    )(qm, kext, vb, means, tiles)


SELECT_DEPTH = 3


def _decode_kernel(pt_ref, qabs_ref, qpe_ref, cnew_ref, kpenew_ref, wkt_ref, qm_ref, lat_hbm, kpet_hbm, k_hbm,
                   o_ref, idx_ref, cbuf, kbuf, cb16, s_scr, mbuf, sums, sem, msem, *, n_pages, tile, pv_tile_keys,
                   unroll):
    s_idx = pl.program_id(0)
    n_seq = pl.num_programs(0)
    slot = s_idx % 2
    past = n_pages * PAGE_SIZE
    dec = cnew_ref.shape[0]
    rows = qabs_ref.shape[0]
    n_tiles = past // tile
    chunk_pages = tile // PAGE_SIZE
    n_full = n_pages // PAGES_PER_BLOCK
    blocks_per_chunk = chunk_pages // PAGES_PER_BLOCK
    width = sums.shape[0]

    def copies(seq, sl, p):
        page = pt_ref[seq * n_pages + p]
        dst = pl.ds(pl.multiple_of(p * PAGE_SIZE, PAGE_SIZE), PAGE_SIZE)
        return (pltpu.make_async_copy(lat_hbm.at[page], cbuf.at[sl, dst], sem.at[0, sl]),
                pltpu.make_async_copy(kpet_hbm.at[page], kbuf.at[sl, :, dst], sem.at[1, sl]))

    def for_pages(seq, sl, fn):
        def body(i, _):
            for u in range(unroll):
                for cp in copies(seq, sl, i * unroll + u):
                    fn(cp)
            return 0
        lax.fori_loop(0, n_pages // unroll, body, 0)

    def chunk_copy(g, p):
        return pltpu.make_async_copy(k_hbm.at[pt_ref[g * chunk_pages + p]], mbuf.at[g % SELECT_DEPTH, p],
                                     msem.at[g % SELECT_DEPTH])

    def start_chunk(g):
        for p in range(chunk_pages):
            chunk_copy(g, p).start()

    @pl.when(s_idx == 0)
    def _():
        for_pages(s_idx, slot, lambda cp: cp.start())
        sums[...] = jnp.zeros(sums.shape, F32)
        for d in range(SELECT_DEPTH - 1):
            start_chunk(d)

    @pl.when(s_idx + 1 < n_seq)
    def _():
        for_pages(s_idx + 1, 1 - slot, lambda cp: cp.start())

    for_pages(s_idx, slot, lambda cp: cp.wait())

    qabs = qabs_ref[...]
    qpe = qpe_ref[...]
    wkt = wkt_ref[...]

    def block_sums(t):
        g = s_idx * n_tiles + t
        ahead = g + SELECT_DEPTH - 1

        @pl.when(ahead < n_seq * n_tiles)
        def _():
            start_chunk(ahead)

        for p in range(chunk_pages):
            chunk_copy(g, p).wait()
        for b in range(blocks_per_chunk):
            x = mbuf[g % SELECT_DEPTH, b * PAGES_PER_BLOCK:(b + 1) * PAGES_PER_BLOCK]
            n = t * blocks_per_chunk + b
            sums[pl.ds(pl.multiple_of(n * HEADS, HEADS), HEADS), :] = jnp.sum(x, axis=(0, 1))

    def scores(c_t, kpe_t):
        kn_t = _dot_nt(wkt, c_t)
        ssq = [jnp.sum(jnp.square(kn_t[h * HEAD_DIM:(h + 1) * HEAD_DIM, :]), axis=0, keepdims=True)
               for h in range(HEADS)]
        inv = lax.rsqrt(jnp.concatenate(ssq, axis=0) * (1.0 / HEAD_DIM) + EPS)
        inv = jnp.concatenate([inv] * dec, axis=0)
        return _dot_nt(qabs, c_t) * inv + _dot(qpe, kpe_t)

    def score_tile(t, _):
        block_sums(t)
        t0 = pl.multiple_of(t * tile, tile)
        c_t = cbuf[slot, pl.ds(t0, tile), :].astype(BF16)
        cb16[pl.ds(t0, tile), :] = c_t
        s_scr[:, pl.ds(t0, tile)] = scores(c_t, kbuf[slot, :, pl.ds(t0, tile)].astype(BF16))
        return 0

    lax.fori_loop(0, n_tiles, score_tile, 0)

    qh, ql = _split(qm_ref[...])
    mh, ml = _split(sums[...] * (1.0 / MOBA_BLOCK))
    gate = _dot_nt(qh, mh) + _dot_nt(qh, ml) + _dot_nt(ql, mh)
    col = lax.broadcasted_iota(jnp.int32, (rows, width), 1)
    row = lax.broadcasted_iota(jnp.int32, (rows, width), 0)
    valid = ((col & (HEADS - 1)) == (row & (HEADS - 1))) & (col < n_full * HEADS)
    gate = jnp.where(valid, gate, NEG)
    rank = jnp.zeros((rows, width), F32)
    for s in range(1, width // HEADS):
        other = pltpu.roll(gate, s * HEADS, axis=1)
        rank = rank + jnp.where(col >= s * HEADS, jnp.where(other >= gate, 1.0, 0.0), jnp.where(other > gate, 1.0, 0.0))
    blk = (col // HEADS).astype(F32)
    lane128 = lax.broadcasted_iota(jnp.int32, (rows, 128), 1)
    picks = jnp.zeros((rows, 128), F32)
    for k in range(MOBA_TOPK):
        pick = jnp.sum(jnp.where(valid & (rank == k), blk, 0.0), axis=-1, keepdims=True)
        picks = jnp.where(lane128 == k, pick, picks)
    idx_ref[...] = picks.astype(jnp.int32)

    c_new = jnp.concatenate([cnew_ref[...], jnp.zeros((PAGE_SIZE - dec, KV_LORA), F32)], axis=0).astype(BF16)
    s_new = scores(c_new, kpenew_ref[...].astype(BF16))
    qrow = lax.broadcasted_iota(jnp.int32, (rows, PAGE_SIZE), 0) // HEADS
    lane = lax.broadcasted_iota(jnp.int32, (rows, PAGE_SIZE), 1)
    s_scr[:, pl.ds(past, PAGE_SIZE)] = jnp.where(lane <= qrow, s_new, NEG)

    s_all = s_scr[...]
    m = jnp.max(s_all, axis=-1, keepdims=True)
    p_all = jnp.exp2(s_all - m)
    l = jnp.sum(p_all, axis=-1, keepdims=True)
    s_scr[...] = p_all

    def pv_tile(t, acc):
        t0 = pl.multiple_of(t * pv_tile_keys, pv_tile_keys)
        return acc + _dot(s_scr[:, pl.ds(t0, pv_tile_keys)].astype(BF16), cb16[pl.ds(t0, pv_tile_keys), :])

    acc = lax.fori_loop(0, past // pv_tile_keys, pv_tile, _dot(s_scr[:, pl.ds(past, PAGE_SIZE)].astype(BF16), c_new))
    o_ref[...] = acc / l


def _largest_tile(total, unit, cap):
    n = total // unit
    best = 1
    for d in range(1, n + 1):
        if n % d == 0 and d * unit <= cap:
            best = d
    return best * unit


def _decode(page_table, qabs, qpe, c_new, kpe_new_t, wkt, qm, lat, kpe_cache_t, k4):
    db, n_pages = page_table.shape
    dec = c_new.shape[1]
    rows = dec * HEADS
    past = n_pages * PAGE_SIZE
    tile = _largest_tile(past, MOBA_BLOCK, 1024)
    chunk_pages = tile // PAGE_SIZE
    n_full = n_pages // PAGES_PER_BLOCK
    width = -(-n_full * HEADS // 128) * 128
    unroll = math.gcd(n_pages, 8)
    assert db * (past // tile) >= SELECT_DEPTH - 1
    per_seq = lambda n: pl.BlockSpec((None, rows, n), lambda s, pt: (s, 0, 0))
    return pl.pallas_call(
        functools.partial(_decode_kernel, n_pages=n_pages, tile=tile, unroll=unroll,
                          pv_tile_keys=_largest_tile(past, PAGE_SIZE, 4096)),
        grid_spec=pltpu.PrefetchScalarGridSpec(
            num_scalar_prefetch=1,
            grid=(db,),
            in_specs=[per_seq(KV_LORA), per_seq(ROPE_DIM),
                      pl.BlockSpec((None, dec, KV_LORA), lambda s, pt: (s, 0, 0)),
                      pl.BlockSpec((None, ROPE_DIM, PAGE_SIZE), lambda s, pt: (s, 0, 0)),
                      pl.BlockSpec(wkt.shape, lambda s, pt: (0, 0)),
                      per_seq(HEAD_DIM),
                      pl.BlockSpec(memory_space=pl.ANY),
                      pl.BlockSpec(memory_space=pl.ANY),
                      pl.BlockSpec(memory_space=pl.ANY)],
            out_specs=[per_seq(KV_LORA), per_seq(128)],
            scratch_shapes=[pltpu.VMEM((2, past, KV_LORA), F32),
                            pltpu.VMEM((2, ROPE_DIM, past), F32),
                            pltpu.VMEM((past, KV_LORA), BF16),
                            pltpu.VMEM((rows, past + PAGE_SIZE), F32),
                            pltpu.VMEM((SELECT_DEPTH, chunk_pages, PAGE_SIZE, HEADS, HEAD_DIM), F32),
                            pltpu.VMEM((width, HEAD_DIM), F32),
                            pltpu.SemaphoreType.DMA((2, 2)),
                            pltpu.SemaphoreType.DMA((SELECT_DEPTH,))]),
        out_shape=[jax.ShapeDtypeStruct((db, rows, KV_LORA), F32), jax.ShapeDtypeStruct((db, rows, 128), jnp.int32)],
        compiler_params=_params(("arbitrary",)),
        name="decode_mla_attn_moba_select",
    )(page_table.reshape(-1), qabs, qpe, c_new, kpe_new_t, wkt, qm, lat, kpe_cache_t, k4)


def _mla_out_kernel(x_ref, wv_ref, o_ref):
    for h in range(HEADS):
        hi, lo = _split(x_ref[:, h * KV_LORA:(h + 1) * KV_LORA])
        wv = wv_ref[:, h * HEAD_DIM:(h + 1) * HEAD_DIM]
        o_ref[:, h * HEAD_DIM:(h + 1) * HEAD_DIM] = (_dot(hi, wv) + _dot(lo, wv)).astype(o_ref.dtype)


def _mla_out(x, wv):
    t = x.shape[0]
    return pl.pallas_call(
        _mla_out_kernel,
        grid=(1,),
        in_specs=[pl.BlockSpec(x.shape, lambda i: (0, 0)), pl.BlockSpec(wv.shape, lambda i: (0, 0))],
        out_specs=pl.BlockSpec((t, HEADS * HEAD_DIM), lambda i: (0, 0)),
        out_shape=jax.ShapeDtypeStruct((t, HEADS * HEAD_DIM), BF16),
        compiler_params=_params(("arbitrary",)),
        name="mla_sample_out",
    )(x, wv)


GATHER_DEPTH = 3


def _moba_sample_kernel(pt_ref, idx_ref, q_ref, kn_ref, vn_ref, last_ref, new_ref, k_hbm, v_hbm, o_ref,
                        kbuf, vbuf, sem, *, n_pages, dec):
    s_idx = pl.program_id(0)
    h = pl.program_id(1)
    g = s_idx * HEADS + h
    n_steps = pl.num_programs(0) * HEADS
    slot = g % GATHER_DEPTH
    n_full = n_pages // PAGES_PER_BLOCK
    n_sel = MOBA_TOPK * dec

    def copies(step, sl, q, k, j):
        seq = step // HEADS
        head = step % HEADS
        blk = idx_ref[step * n_sel + q * MOBA_TOPK + k]
        page = pt_ref[seq * n_pages + blk * PAGES_PER_BLOCK + j]
        dst = pl.ds((k * PAGES_PER_BLOCK + j) * PAGE_SIZE, PAGE_SIZE)
        return (pltpu.make_async_copy(k_hbm.at[page, :, head, :], kbuf.at[sl, q, dst], sem.at[0, sl]),
                pltpu.make_async_copy(v_hbm.at[page, :, head, :], vbuf.at[sl, q, dst], sem.at[1, sl]))

    def for_all(step, sl, fn):
        for q in range(dec):
            for k in range(MOBA_TOPK):
                for j in range(PAGES_PER_BLOCK):
                    for cp in copies(step, sl, q, k, j):
                        fn(cp)

    @pl.when(g == 0)
    def _():
        for d in range(GATHER_DEPTH - 1):
            for_all(d, d, lambda cp: cp.start())

    ahead = g + GATHER_DEPTH - 1

    @pl.when(ahead < n_steps)
    def _():
        for_all(ahead, ahead % GATHER_DEPTH, lambda cp: cp.start())

    for_all(g, slot, lambda cp: cp.wait())

    seg = MOBA_TOPK * MOBA_BLOCK
    total = dec * seg
    q8 = (q_ref[...] * (MOBA_SCALE * LOG2E)).astype(BF16)
    kn = kn_ref[...].astype(BF16)
    vn = vn_ref[...].astype(BF16)
    k_all = kbuf[slot].reshape(total, HEAD_DIM).astype(BF16)
    v_all = vbuf[slot].reshape(total, HEAD_DIM).astype(BF16)
    last = last_ref[...]
    bias = [jnp.where(idx_ref[g * n_sel + i] == n_full - 1, last, 0.0) for i in range(n_sel)]
    row = lax.broadcasted_iota(jnp.int32, (8, total), 0)
    col = lax.broadcasted_iota(jnp.int32, (8, total), 1)
    own = (col >= row * seg) & (col < (row + 1) * seg)
    s = jnp.where(own, _dot_nt(q8, k_all) + jnp.concatenate(bias, axis=-1), NEG)
    sn = _dot_nt(q8, kn) + new_ref[:, :8]
    m = jnp.maximum(jnp.max(s, axis=-1, keepdims=True), jnp.max(sn, axis=-1, keepdims=True))
    p = jnp.exp2(s - m)
    pn = jnp.exp2(sn - m)
    l = jnp.sum(p, axis=-1, keepdims=True) + jnp.sum(pn, axis=-1, keepdims=True)
    o = (_dot(p.astype(BF16), v_all) + _dot(pn.astype(BF16), vn)) / l
    o_ref[...] = o[:dec, :]


def _moba_sample(page_table, idx, q8, kn8, vn8, bias_last, bias_new, k4, v4, dec):
    db, n_pages = page_table.shape
    per = lambda: pl.BlockSpec((None, None, 8, HEAD_DIM), lambda s, h, pt, ix: (s, h, 0, 0))
    return pl.pallas_call(
        functools.partial(_moba_sample_kernel, n_pages=n_pages, dec=dec),
        grid_spec=pltpu.PrefetchScalarGridSpec(
            num_scalar_prefetch=2,
            grid=(db, HEADS),
            in_specs=[per(), per(), per(),
                      pl.BlockSpec((None, 8, MOBA_BLOCK), lambda s, h, pt, ix: (h, 0, 0)),
                      pl.BlockSpec((None, 8, 128), lambda s, h, pt, ix: (h, 0, 0)),
                      pl.BlockSpec(memory_space=pl.ANY),
                      pl.BlockSpec(memory_space=pl.ANY)],
            out_specs=pl.BlockSpec((None, dec, HEAD_DIM), lambda s, h, pt, ix: (s, 0, h)),
            scratch_shapes=[pltpu.VMEM((GATHER_DEPTH, dec, MOBA_TOPK * MOBA_BLOCK, HEAD_DIM), F32),
                            pltpu.VMEM((GATHER_DEPTH, dec, MOBA_TOPK * MOBA_BLOCK, HEAD_DIM), F32),
                            pltpu.SemaphoreType.DMA((2, GATHER_DEPTH))]),
        out_shape=jax.ShapeDtypeStruct((db, dec, HEADS * HEAD_DIM), F32),
        compiler_params=_params(("arbitrary", "arbitrary")),
        name="moba_sample_attn",
    )(page_table.reshape(-1), idx, q8, kn8, vn8, bias_last, bias_new, k4, v4)


def _merge_ffn_kernel(x_ref, oa_ref, ob_ref, ga_ref, wg_ref, wo_ref, gf_ref, wgu_ref, wd_ref, y_ref):
    x = x_ref[...]
    xn = _rms(x, ga_ref[...]).astype(BF16)
    g = _dot(xn, wg_ref[...])
    mix = jax.nn.sigmoid(g[:, :D_MODEL]) * oa_ref[...].astype(F32) + jax.nn.sigmoid(g[:, D_MODEL:]) * ob_ref[...].astype(F32)
    hres = x + _dot(mix.astype(BF16), wo_ref[...])
    gu = _dot(_rms(hres, gf_ref[...]).astype(BF16), wgu_ref[...])
    d_ff = wd_ref.shape[0]
    act = jax.nn.silu(gu[:, :d_ff]) * gu[:, d_ff:]
    y_ref[...] = hres + _dot(act.astype(BF16), wd_ref[...])


def _merge_ffn(x, o_a, o_b, w):
    t = x.shape[0]
    tm = ROW_TILE
    row = lambda: pl.BlockSpec((tm, D_MODEL), lambda i: (i, 0))
    consts = [w["norm_attn"], w["w_g"], w["w_o"], w["norm_ffn"], w["w_gate_up"], w["w_down"]]
    return pl.pallas_call(
        _merge_ffn_kernel,
        grid=(t // tm,),
        in_specs=[row(), row(), row()] + [_const_spec(a.shape) for a in consts],
        out_specs=row(),
        out_shape=jax.ShapeDtypeStruct((t, D_MODEL), F32),
        compiler_params=_params(("parallel",)),
        name="merge_ffn",
    )(x, o_a, o_b, *consts)


def _rope_tables(pos):
    half = ROPE_DIM // 2
    freqs = ROPE_THETA ** (-jnp.arange(half, dtype=F32) / half)
    ang = pos.astype(F32)[:, None] * freqs
    cos, sin = jnp.cos(ang), jnp.sin(ang)
    return jnp.concatenate([cos, cos], axis=-1), jnp.concatenate([-sin, sin], axis=-1)


def _layer_weights(l, norm_attn, w_in, q_a_norm, w_q_b, kv_a_norm, w_kv_b, mla_qn_gain, mla_qr_gain, mla_kn_gain,
                   mla_kr_gain, moba_q_gain, moba_k_gain, w_o, norm_ffn, w_gate_up, w_down):
    width = HEADS * HEAD_DIM
    a_end = Q_LORA + KV_LORA + ROPE_DIM
    row = lambda a: a[l].reshape(1, -1)
    win = w_in[l].astype(BF16)
    wqb = w_q_b[l].reshape(Q_LORA, HEADS, HEAD_DIM + ROPE_DIM)
    wkv = w_kv_b[l].reshape(KV_LORA, HEADS, 2 * HEAD_DIM)
    return {
        "norm_attn": row(norm_attn), "q_a_norm": row(q_a_norm), "kv_a_norm": row(kv_a_norm),
        "qn_gain": row(mla_qn_gain), "qr_gain": row(mla_qr_gain), "kn_gain": row(mla_kn_gain),
        "kr_gain": row(mla_kr_gain), "moba_q_gain": row(moba_q_gain), "moba_k_gain": row(moba_k_gain),
        "norm_ffn": row(norm_ffn),
        "w_a": win[:, :a_end], "w_m": win[:, a_end:a_end + 3 * width], "w_g": win[:, a_end + 3 * width:],
        "w_q_b": jnp.concatenate([wqb[:, :, :HEAD_DIM].reshape(Q_LORA, -1),
                                  wqb[:, :, HEAD_DIM:].reshape(Q_LORA, -1)], axis=-1).astype(BF16),
        "w_kv": jnp.concatenate([wkv[:, :, :HEAD_DIM].reshape(KV_LORA, -1),
                                 wkv[:, :, HEAD_DIM:].reshape(KV_LORA, -1)], axis=-1).astype(BF16),
        "w_o": w_o[l].astype(BF16), "w_gate_up": w_gate_up[l].astype(BF16), "w_down": w_down[l].astype(BF16),
    }


def kernel(x_prompt, x_sample, cache_mla_latent, cache_mla_kpe, cache_moba_k, cache_moba_v, page_table, norm_attn, w_in, q_a_norm, w_q_b, kv_a_norm, w_kv_b, mla_qn_gain, mla_qr_gain, mla_kn_gain, mla_kr_gain, moba_q_gain, moba_k_gain, t5_table, w_o, norm_ffn, w_gate_up, w_down):
    batch, seq, _ = x_prompt.shape
    db, dec, _ = x_sample.shape
    depth = w_in.shape[0]
    n_pool = cache_mla_latent.shape[1]
    n_pages = page_table.shape[1]
    past = n_pages * PAGE_SIZE
    width = HEADS * HEAD_DIM
    assert seq % ROW_TILE == 0 and (db * dec) % ROW_TILE == 0 and ROW_TILE % dec == 0
    assert past % MOBA_BLOCK == 0 and past // MOBA_BLOCK >= MOBA_TOPK and dec <= 8

    cos_p, sin_p = _rope_tables(jnp.arange(seq))
    cos_s, sin_s = _rope_tables(past + jnp.arange(ROW_TILE) % dec)
    tiles, bias_last, bias_new = _t5_tables(t5_table, 8)

    yp = x_prompt.reshape(batch * seq, D_MODEL)
    ys = x_sample.reshape(db * dec, D_MODEL)
    outs = [[] for _ in range(8)]
    for l in range(depth):
        w = _layer_weights(l, norm_attn, w_in, q_a_norm, w_q_b, kv_a_norm, w_kv_b, mla_qn_gain, mla_qr_gain,
                           mla_kn_gain, mla_kr_gain, moba_q_gain, moba_k_gain, w_o, norm_ffn, w_gate_up, w_down)
        c, kpe, qcat, kcat, v = _mla_proj(yp, w, cos_p, sin_p, sample=False)
        qm, km, vm, kext, vb, means = _moba_proj(yp, w, blocks_per_seq=seq // MOBA_BLOCK)
        o_a = _mla_prompt(qcat, kcat, v, batch, seq)
        o_b = _moba_prompt(qm, kext, vb, means.reshape(batch, seq // MOBA_BLOCK, width), tiles, batch, seq)
        outs[0].append(c.reshape(batch, seq, KV_LORA))
        outs[1].append(kpe.reshape(batch, seq, ROPE_DIM))
        outs[2].append(km.reshape(batch, seq, HEADS, HEAD_DIM))
        outs[3].append(vm.reshape(batch, seq, HEADS, HEAD_DIM))
        yp = _merge_ffn(yp, o_a, o_b, w)
        c, kpe, qabs, qpe = _mla_proj(ys, w, cos_s, sin_s, sample=True)
        qm, km, vm = _moba_proj(ys, w)
        kpe_new_t = jnp.pad(jnp.swapaxes(kpe.reshape(db, dec, ROPE_DIM), 1, 2), ((0, 0), (0, 0), (0, PAGE_SIZE - dec)))
        k4 = cache_moba_k[l]
        v4 = cache_moba_v[l]
        lat, idx = _decode(page_table, qabs.reshape(db, dec * HEADS, KV_LORA), qpe.reshape(db, dec * HEADS, ROPE_DIM),
                           c.reshape(db, dec, KV_LORA), kpe_new_t, w["w_kv"][:, :width].T,
                           qm.reshape(db, dec * HEADS, HEAD_DIM),
                           cache_mla_latent[l], jnp.swapaxes(cache_mla_kpe[l], 1, 2), k4)
        o_a = _mla_out(lat.reshape(db * dec, HEADS * KV_LORA), w["w_kv"][:, width:])
        idx = jnp.transpose(idx[:, :, :MOBA_TOPK].reshape(db, dec, HEADS, MOBA_TOPK), (0, 2, 1, 3)).reshape(-1)
        to_heads = lambda a: jnp.pad(jnp.transpose(a.reshape(db, dec, HEADS, HEAD_DIM), (0, 2, 1, 3)),
                                     ((0, 0), (0, 0), (0, 8 - dec), (0, 0)))
        o_b = _moba_sample(page_table, idx, to_heads(qm), to_heads(km), to_heads(vm), bias_last, bias_new, k4, v4, dec)
        outs[4].append(c.reshape(db, dec, KV_LORA))
        outs[5].append(kpe.reshape(db, dec, ROPE_DIM))
        outs[6].append(km.reshape(db, dec, HEADS, HEAD_DIM))
        outs[7].append(vm.reshape(db, dec, HEADS, HEAD_DIM))
        ys = _merge_ffn(ys, o_a, o_b.reshape(db * dec, width), w)
    return (yp.reshape(batch, seq, D_MODEL), ys.reshape(db, dec, D_MODEL)) + tuple(jnp.stack(o, axis=0) for o in outs)
```

```python
import functools
import math

import numpy as np
import jax
import jax.numpy as jnp
from jax import lax
from jax.experimental import pallas as pl
from jax.experimental.pallas import tpu as pltpu

F32 = jnp.float32
BF16 = jnp.bfloat16

D_MODEL = 1024
HEADS = 8
HEAD_DIM = 128
ROPE_DIM = 64
Q_LORA = 256
KV_LORA = 256
QK_PAD = 256
ROPE_THETA = 10000.0
MLA_SCALE = (HEAD_DIM + ROPE_DIM) ** -0.5
MOBA_BLOCK = 256
MOBA_TOPK = 3
MOBA_SCALE = HEAD_DIM ** -0.5
PAGE_SIZE = 128
PAGES_PER_BLOCK = MOBA_BLOCK // PAGE_SIZE
T5_BUCKETS = 32
T5_MAX_DIST = 128
EPS = 1e-6
NEG = -1e30
LOG2E = math.log2(math.e)
ROW_TILE = 256
MLA_TILE = 512
VMEM_LIMIT = 56 * 1024 * 1024

_NT = (((1,), (1,)), ((), ()))


def _dot(a, b):
    return jnp.dot(a, b, preferred_element_type=F32)


def _dot_nt(a, b):
    return lax.dot_general(a, b, _NT, preferred_element_type=F32)


def _split(x):
    hi = x.astype(BF16)
    lo = (x - hi.astype(F32)).astype(BF16)
    return hi, lo


def _rms(x, g):
    ms = jnp.mean(x * x, axis=-1, keepdims=True)
    return x * lax.rsqrt(ms + EPS) * g


def _rope(x, cosf, sinf):
    half = ROPE_DIM // 2
    swapped = jnp.concatenate([x[:, half:], x[:, :half]], axis=-1)
    return x * cosf + swapped * sinf


def _const_spec(shape):
    n = len(shape)
    return pl.BlockSpec(shape, lambda *_: (0,) * n, pipeline_mode=pl.Buffered(1))


def _params(sem):
    return pltpu.CompilerParams(dimension_semantics=sem, vmem_limit_bytes=VMEM_LIMIT)


def _t5_bucket_upper_bounds(max_dist):
    d = np.arange(max_dist + 1)
    max_exact = T5_BUCKETS // 2
    dd = np.maximum(d, max_exact).astype(np.float32)
    large = max_exact + (np.log(dd / np.float32(max_exact)) / np.float32(math.log(T5_MAX_DIST / max_exact))
                         * np.float32(T5_BUCKETS - max_exact)).astype(np.int32)
    bucket = np.where(d < max_exact, d, np.minimum(large, T5_BUCKETS - 1))
    assert bucket[-1] == T5_BUCKETS - 1
    return [int(d[bucket == b].max()) if np.any(bucket == b) else None for b in range(T5_BUCKETS)]


def _t5_bias(dist, h, t5_ref, bounds):
    res = jnp.full(dist.shape, t5_ref[T5_BUCKETS - 1, h], F32)
    for b in range(T5_BUCKETS - 2, -1, -1):
        if bounds[b] is not None:
            res = jnp.where(dist <= bounds[b], t5_ref[b, h], res)
    return res


def _t5_kernel(t5_ref, tiles_ref, last_ref, new_ref, *, bounds, dec_seq):
    h = pl.program_id(0)
    far = t5_ref[T5_BUCKETS - 1, h]
    rel = lambda dist: (_t5_bias(dist, h, t5_ref, bounds) - far) * LOG2E
    r = lax.broadcasted_iota(jnp.int32, (MOBA_BLOCK, MOBA_BLOCK), 0)
    c = lax.broadcasted_iota(jnp.int32, (MOBA_BLOCK, MOBA_BLOCK), 1)
    tiles_ref[0] = jnp.where(c <= r, rel(r - c), NEG)
    tiles_ref[1] = rel(MOBA_BLOCK + r - c)
    q = lax.broadcasted_iota(jnp.int32, (dec_seq, MOBA_BLOCK), 0)
    j = lax.broadcasted_iota(jnp.int32, (dec_seq, MOBA_BLOCK), 1)
    last_ref[...] = rel(MOBA_BLOCK + q - j)
    qn = lax.broadcasted_iota(jnp.int32, (dec_seq, 128), 0)
    jn = lax.broadcasted_iota(jnp.int32, (dec_seq, 128), 1)
    new_ref[...] = jnp.where(jn <= qn, rel(qn - jn), NEG)


def _t5_tables(t5_table, dec_seq):
    bounds = _t5_bucket_upper_bounds(2 * MOBA_BLOCK + dec_seq)
    return pl.pallas_call(
        functools.partial(_t5_kernel, bounds=bounds, dec_seq=dec_seq),
        grid=(HEADS,),
        in_specs=[pl.BlockSpec(memory_space=pltpu.SMEM)],
        out_specs=[pl.BlockSpec((None, 2, MOBA_BLOCK, MOBA_BLOCK), lambda h: (h, 0, 0, 0)),
                   pl.BlockSpec((None, dec_seq, MOBA_BLOCK), lambda h: (h, 0, 0)),
                   pl.BlockSpec((None, dec_seq, 128), lambda h: (h, 0, 0))],
        out_shape=[jax.ShapeDtypeStruct((HEADS, 2, MOBA_BLOCK, MOBA_BLOCK), F32),
                   jax.ShapeDtypeStruct((HEADS, dec_seq, MOBA_BLOCK), F32),
                   jax.ShapeDtypeStruct((HEADS, dec_seq, 128), F32)],
        compiler_params=_params(("arbitrary",)),
        name="t5_bias",
    )(t5_table)


def _mla_proj_kernel(x_ref, g_ref, wa_ref, qan_ref, wqb_ref, kvn_ref, qng_ref, qrg_ref, krg_ref, kng_ref,
                     cos_ref, sin_ref, wkv_ref, c_ref, kpe_ref, *outs, sample):
    tm = x_ref.shape[0]
    xn = _rms(x_ref[...], g_ref[...]).astype(BF16)
    pa = _dot(xn, wa_ref[...])
    cosf, sinf = cos_ref[...], sin_ref[...]
    c = _rms(pa[:, Q_LORA:Q_LORA + KV_LORA], kvn_ref[...])
    kpe = _rope(_rms(pa[:, Q_LORA + KV_LORA:], krg_ref[...]), cosf, sinf)
    c_ref[...] = c
    kpe_ref[...] = kpe
    q = _dot(_rms(pa[:, :Q_LORA], qan_ref[...]).astype(BF16), wqb_ref[...])
    nope_w = HEADS * HEAD_DIM
    pad = jnp.zeros((tm, QK_PAD - HEAD_DIM - ROPE_DIM), F32)
    if sample:
        qabs_ref, qpe_ref = outs
        qpes = []
    else:
        qcat_ref, kcat_ref, v_ref = outs
        kv = _dot(c.astype(BF16), wkv_ref[...])
        v_ref[...] = kv[:, nope_w:].astype(BF16)
    for h in range(HEADS):
        qn = _rms(q[:, h * HEAD_DIM:(h + 1) * HEAD_DIM], qng_ref[...]) * (MLA_SCALE * LOG2E)
        qp = _rope(_rms(q[:, nope_w + h * ROPE_DIM:nope_w + (h + 1) * ROPE_DIM], qrg_ref[...]), cosf, sinf) * (MLA_SCALE * LOG2E)
        if sample:
            hi, lo = _split(qn * kng_ref[...])
            wk = wkv_ref[:, h * HEAD_DIM:(h + 1) * HEAD_DIM]
            qabs_ref[:, h * KV_LORA:(h + 1) * KV_LORA] = (_dot_nt(hi, wk) + _dot_nt(lo, wk)).astype(BF16)
            qpes.append(qp)
        else:
            kn = _rms(kv[:, h * HEAD_DIM:(h + 1) * HEAD_DIM], kng_ref[...])
            qcat_ref[:, h * QK_PAD:(h + 1) * QK_PAD] = jnp.concatenate([qn, qp, pad], axis=-1).astype(BF16)
            kcat_ref[:, h * QK_PAD:(h + 1) * QK_PAD] = jnp.concatenate([kn, kpe, pad], axis=-1).astype(BF16)
    if sample:
        qpe_ref[...] = jnp.concatenate(qpes, axis=-1).astype(BF16)


def _mla_proj(x, w, cosf, sinf, *, sample):
    t = x.shape[0]
    tm = ROW_TILE
    n_pos = cosf.shape[0] // tm
    row = lambda width: pl.BlockSpec((tm, width), lambda i: (i, 0))
    pos = pl.BlockSpec((tm, ROPE_DIM), lambda i: (i % n_pos, 0))
    consts = [w["norm_attn"], w["w_a"], w["q_a_norm"], w["w_q_b"], w["kv_a_norm"], w["qn_gain"], w["qr_gain"],
              w["kr_gain"], w["kn_gain"]]
    if sample:
        widths, dtypes = [KV_LORA, ROPE_DIM, HEADS * KV_LORA, HEADS * ROPE_DIM], [F32, F32, BF16, BF16]
    else:
        widths, dtypes = [KV_LORA, ROPE_DIM, HEADS * QK_PAD, HEADS * QK_PAD, HEADS * HEAD_DIM], [F32, F32, BF16, BF16, BF16]
    return pl.pallas_call(
        functools.partial(_mla_proj_kernel, sample=sample),
        grid=(t // tm,),
        in_specs=[row(D_MODEL)] + [_const_spec(a.shape) for a in consts] + [pos, pos, _const_spec(w["w_kv"].shape)],
        out_specs=[row(n) for n in widths],
        out_shape=[jax.ShapeDtypeStruct((t, n), d) for n, d in zip(widths, dtypes)],
        compiler_params=_params(("parallel",)),
        name="mla_proj_sample" if sample else "mla_proj_prompt",
    )(x, *consts, cosf, sinf, w["w_kv"])


def _moba_proj_kernel(x_ref, g_ref, wm_ref, qg_ref, kg_ref, q_ref, k_ref, v_ref, *prompt_refs, blocks_per_seq):
    tm = x_ref.shape[0]
    xn = _rms(x_ref[...], g_ref[...]).astype(BF16)
    pm = _dot(xn, wm_ref[...])
    w = HEADS * HEAD_DIM
    v = pm[:, 2 * w:]
    v_ref[...] = v
    if prompt_refs:
        kext_ref, vb_ref, mean_ref = prompt_refs
        vb_ref[...] = v.astype(BF16)
        lane = lax.broadcasted_iota(jnp.int32, (tm, QK_PAD - HEAD_DIM), 1)
        onehot = jnp.where(lane == pl.program_id(0) % blocks_per_seq, 1.0, 0.0).astype(BF16)
    for h in range(HEADS):
        sl = slice(h * HEAD_DIM, (h + 1) * HEAD_DIM)
        q_ref[:, sl] = _rms(pm[:, h * HEAD_DIM:(h + 1) * HEAD_DIM], qg_ref[...])
        k = _rms(pm[:, w + h * HEAD_DIM:w + (h + 1) * HEAD_DIM], kg_ref[...])
        k_ref[:, sl] = k
        if prompt_refs:
            kext_ref[:, h * QK_PAD:(h + 1) * QK_PAD] = jnp.concatenate([k.astype(BF16), onehot], axis=-1)
            mean_ref[:, sl] = jnp.mean(k, axis=0, keepdims=True)


def _moba_proj(x, w, *, blocks_per_seq=None):
    t = x.shape[0]
    tm = ROW_TILE
    width = HEADS * HEAD_DIM
    prompt = blocks_per_seq is not None
    row = lambda n=width: pl.BlockSpec((tm, n), lambda i: (i, 0))
    consts = [w["norm_attn"], w["w_m"], w["moba_q_gain"], w["moba_k_gain"]]
    out_specs = [row() for _ in range(3)]
    out_shape = [jax.ShapeDtypeStruct((t, width), F32) for _ in range(3)]
    if prompt:
        assert tm == MOBA_BLOCK and blocks_per_seq <= QK_PAD - HEAD_DIM
        out_specs += [row(HEADS * QK_PAD), row(), pl.BlockSpec((None, 1, width), lambda i: (i, 0, 0))]
        out_shape += [jax.ShapeDtypeStruct((t, HEADS * QK_PAD), BF16), jax.ShapeDtypeStruct((t, width), BF16),
                      jax.ShapeDtypeStruct((t // tm, 1, width), F32)]
    return pl.pallas_call(
        functools.partial(_moba_proj_kernel, blocks_per_seq=blocks_per_seq),
        grid=(t // tm,),
        in_specs=[pl.BlockSpec((tm, D_MODEL), lambda i: (i, 0))] + [_const_spec(a.shape) for a in consts],
        out_specs=out_specs,
        out_shape=out_shape,
        compiler_params=_params(("parallel",)),
        name="moba_proj_prompt" if prompt else "moba_proj_sample",
    )(x, *consts)


def _online_softmax_step(s, v, m, l, acc):
    m_new = jnp.maximum(m, jnp.max(s, axis=-1, keepdims=True))
    a = jnp.exp2(m - m_new)
    p = jnp.exp2(s - m_new)
    return m_new, a * l + jnp.sum(p, axis=-1, keepdims=True), a * acc + _dot(p.astype(BF16), v)


def _mla_prompt_kernel(q_ref, k_ref, v_ref, o_ref):
    qi = pl.program_id(2)
    tq = q_ref.shape[0]
    q = q_ref[...]
    d0 = pl.multiple_of(qi * tq, tq)
    s = _dot_nt(q, k_ref[pl.ds(d0, tq), :])
    r = lax.broadcasted_iota(jnp.int32, (tq, tq), 0)
    c = lax.broadcasted_iota(jnp.int32, (tq, tq), 1)
    s = jnp.where(c <= r, s, NEG)
    m = jnp.max(s, axis=-1, keepdims=True)
    p = jnp.exp2(s - m)
    l = jnp.sum(p, axis=-1, keepdims=True)
    acc = _dot(p.astype(BF16), v_ref[pl.ds(d0, tq), :])

    def body(j, carry):
        j0 = pl.multiple_of(j * tq, tq)
        return _online_softmax_step(_dot_nt(q, k_ref[pl.ds(j0, tq), :]), v_ref[pl.ds(j0, tq), :], *carry)

    m, l, acc = lax.fori_loop(0, qi, body, (m, l, acc))
    o_ref[...] = (acc / l).astype(o_ref.dtype)


def _mla_prompt(qcat, kcat, v, batch, seq):
    tq = min(MLA_TILE, seq)
    nq = seq // tq
    return pl.pallas_call(
        _mla_prompt_kernel,
        grid=(batch, HEADS, nq),
        in_specs=[pl.BlockSpec((tq, QK_PAD), lambda b, h, i: (b * nq + i, h)),
                  pl.BlockSpec((seq, QK_PAD), lambda b, h, i: (b, h)),
                  pl.BlockSpec((seq, HEAD_DIM), lambda b, h, i: (b, h))],
        out_specs=pl.BlockSpec((tq, HEAD_DIM), lambda b, h, i: (b * nq + i, h)),
        out_shape=jax.ShapeDtypeStruct((batch * seq, HEADS * HEAD_DIM), BF16),
        compiler_params=_params(("parallel", "parallel", "arbitrary")),
        name="mla_prompt_attn",
    )(qcat, kcat, v)


def _moba_prompt_kernel(q_ref, k_ref, v_ref, mean_ref, tiles_ref, o_ref, sa_ref, sb_ref, *, group):
    i = pl.program_id(2)
    nb = mean_ref.shape[0]
    blk = MOBA_BLOCK
    pad = QK_PAD - HEAD_DIM
    qf = q_ref[...]
    qs = (qf * (MOBA_SCALE * LOG2E)).astype(BF16)
    d0 = pl.multiple_of(i * blk, blk)
    p0 = pl.multiple_of(jnp.maximum(i - 1, 0) * blk, blk)
    s_own = _dot_nt(qs, k_ref[pl.ds(d0, blk), :HEAD_DIM]) + tiles_ref[0]
    s_prev = _dot_nt(qs, k_ref[pl.ds(p0, blk), :HEAD_DIM]) + (tiles_ref[1] + jnp.where(i == 0, NEG, 0.0))

    qh, ql = _split(qf)
    mh, ml = _split(mean_ref[...])
    gate = _dot_nt(mh, qh) + _dot_nt(ml, qh) + _dot_nt(mh, ql)
    row = lax.broadcasted_iota(jnp.int32, (nb, blk), 0)
    gate = jnp.where(row < i, gate, NEG)
    rank = jnp.zeros((nb, blk), F32)
    for jp in range(nb):
        gj = gate[jp:jp + 1, :]
        rank = rank + jnp.where(row > jp, jnp.where(gj >= gate, 1.0, 0.0), jnp.where(gj > gate, 1.0, 0.0))
    sel = (rank < MOBA_TOPK) & (row < i)
    prev = jnp.where((row == i - 1) & sel, 0.0, NEG)
    far = jnp.where((row < i - 1) & sel, 0.0, NEG)
    fill = jnp.zeros((pad - nb, blk), F32)
    prev_t = jnp.concatenate([prev, fill], axis=0).T.astype(BF16)
    far_t = jnp.concatenate([far, fill], axis=0).T.astype(BF16)
    q_far = jnp.concatenate([qs, far_t], axis=-1)

    s_prev = s_prev + _dot_nt(prev_t, k_ref[pl.ds(p0, blk), HEAD_DIM:])
    m = jnp.maximum(jnp.max(s_own, axis=-1, keepdims=True), jnp.max(s_prev, axis=-1, keepdims=True))
    p_own = jnp.exp2(s_own - m)
    p_prev = jnp.exp2(s_prev - m)
    l = jnp.sum(p_own, axis=-1, keepdims=True) + jnp.sum(p_prev, axis=-1, keepdims=True)
    acc = _dot(p_own.astype(BF16), v_ref[pl.ds(d0, blk), :]) + _dot(p_prev.astype(BF16), v_ref[pl.ds(p0, blk), :])

    span = group * blk

    def logits(g):
        return _dot_nt(q_far, k_ref[pl.ds(pl.multiple_of(g * span, span), span), :])

    def consume(s_ref, g, carry):
        return _online_softmax_step(s_ref[...], v_ref[pl.ds(pl.multiple_of(g * span, span), span), :], *carry)

    n_spans = (jnp.maximum(i - 1, 0) + group - 1) // group
    n_pairs = (n_spans + 1) // 2
    last = nb // group - 1
    sa_ref[...] = logits(0)

    def body(pp, carry):
        sb_ref[...] = logits(jnp.minimum(2 * pp + 1, last))
        carry = consume(sa_ref, 2 * pp, carry)
        sa_ref[...] = logits(jnp.minimum(2 * pp + 2, last))
        return consume(sb_ref, jnp.minimum(2 * pp + 1, last), carry)

    m, l, acc = lax.fori_loop(0, n_pairs, body, (m, l, acc))
    o_ref[...] = (acc / l).astype(o_ref.dtype)


def _moba_prompt(qm, kext, vb, means, tiles, batch, seq):
    assert seq % MOBA_BLOCK == 0
    nb = seq // MOBA_BLOCK
    blk = MOBA_BLOCK
    group = math.gcd(nb, 2)
    return pl.pallas_call(
        functools.partial(_moba_prompt_kernel, group=group),
        scratch_shapes=[pltpu.VMEM((blk, group * blk), F32), pltpu.VMEM((blk, group * blk), F32)],
        grid=(batch, HEADS, nb),
        in_specs=[pl.BlockSpec((blk, HEAD_DIM), lambda b, h, i: (b * nb + i, h)),
                  pl.BlockSpec((seq, QK_PAD), lambda b, h, i: (b, h)),
                  pl.BlockSpec((seq, HEAD_DIM), lambda b, h, i: (b, h)),
                  pl.BlockSpec((None, nb, HEAD_DIM), lambda b, h, i: (b, 0, h)),
                  pl.BlockSpec((None, 2, blk, blk), lambda b, h, i: (h, 0, 0, 0))],
        out_specs=pl.BlockSpec((blk, HEAD_DIM), lambda b, h, i: (b * nb + i, h)),
        out_shape=jax.ShapeDtypeStruct((batch * seq, HEADS * HEAD_DIM), BF16),
        compiler_params=_params(("parallel", "parallel", "arbitrary")),
        name="moba_prompt_attn",
    )(qm, kext, vb, means, tiles)


SELECT_DEPTH = 4


def _decode_kernel(pt_ref, qabs_ref, qpe_ref, cnew_ref, kpenew_ref, wkt_ref, qm_ref, lat_hbm, kpet_hbm, k_hbm,
                   o_ref, idx_ref, cbuf, kbuf, cb16, s_scr, mbuf, sums, sem, msem, *, n_pages, tile, pv_tile_keys,
                   unroll):
    s_idx = pl.program_id(0)
    n_seq = pl.num_programs(0)
    slot = s_idx % 2
    past = n_pages * PAGE_SIZE
    dec = cnew_ref.shape[0]
    rows = qabs_ref.shape[0]
    n_tiles = past // tile
    chunk_pages = tile // PAGE_SIZE
    n_full = n_pages // PAGES_PER_BLOCK
    blocks_per_chunk = chunk_pages // PAGES_PER_BLOCK
    width = sums.shape[0]

    def copies(seq, sl, p):
        page = pt_ref[seq * n_pages + p]
        dst = pl.ds(pl.multiple_of(p * PAGE_SIZE, PAGE_SIZE), PAGE_SIZE)
        return (pltpu.make_async_copy(lat_hbm.at[page], cbuf.at[sl, dst], sem.at[0, sl]),
                pltpu.make_async_copy(kpet_hbm.at[page], kbuf.at[sl, :, dst], sem.at[1, sl]))

    def for_pages(seq, sl, fn):
        def body(i, _):
            for u in range(unroll):
                for cp in copies(seq, sl, i * unroll + u):
                    fn(cp)
            return 0
        lax.fori_loop(0, n_pages // unroll, body, 0)

    def chunk_copy(g, p):
        return pltpu.make_async_copy(k_hbm.at[pt_ref[g * chunk_pages + p]], mbuf.at[g % SELECT_DEPTH, p],
                                     msem.at[g % SELECT_DEPTH])

    def start_chunk(g):
        for p in range(chunk_pages):
            chunk_copy(g, p).start()

    @pl.when(s_idx == 0)
    def _():
        for_pages(s_idx, slot, lambda cp: cp.start())
        sums[...] = jnp.zeros(sums.shape, F32)
        for d in range(SELECT_DEPTH - 1):
            start_chunk(d)

    for_pages(s_idx, slot, lambda cp: cp.wait())

    qabs = qabs_ref[...]
    qpe = qpe_ref[...]
    wkt = wkt_ref[...]

    def block_sums(t):
        g = s_idx * n_tiles + t
        ahead = g + SELECT_DEPTH - 1

        @pl.when(ahead < n_seq * n_tiles)
        def _():
            start_chunk(ahead)

        @pl.when(s_idx + 1 < n_seq)
        def _():
            for p in range(chunk_pages):
                for cp in copies(s_idx + 1, 1 - slot, t * chunk_pages + p):
                    cp.start()

        for p in range(chunk_pages):
            chunk_copy(g, p).wait()
        for b in range(blocks_per_chunk):
            x = mbuf[g % SELECT_DEPTH, b * PAGES_PER_BLOCK:(b + 1) * PAGES_PER_BLOCK]
            n = t * blocks_per_chunk + b
            sums[pl.ds(pl.multiple_of(n * HEADS, HEADS), HEADS), :] = jnp.sum(x, axis=(0, 1))

    def scores(c_t, kpe_t):
        kn_t = _dot_nt(wkt, c_t)
        ssq = [jnp.sum(jnp.square(kn_t[h * HEAD_DIM:(h + 1) * HEAD_DIM, :]), axis=0, keepdims=True)
               for h in range(HEADS)]
        inv = lax.rsqrt(jnp.concatenate(ssq, axis=0) * (1.0 / HEAD_DIM) + EPS)
        inv = jnp.concatenate([inv] * dec, axis=0)
        return _dot_nt(qabs, c_t) * inv + _dot(qpe, kpe_t)

    def score_tile(t, _):
        block_sums(t)
        t0 = pl.multiple_of(t * tile, tile)
        c_t = cbuf[slot, pl.ds(t0, tile), :].astype(BF16)
        cb16[pl.ds(t0, tile), :] = c_t
        s_scr[:, pl.ds(t0, tile)] = scores(c_t, kbuf[slot, :, pl.ds(t0, tile)].astype(BF16))
        return 0

    lax.fori_loop(0, n_tiles, score_tile, 0)

    qh, ql = _split(qm_ref[...])
    mh, ml = _split(sums[...] * (1.0 / MOBA_BLOCK))
    gate = _dot_nt(qh, mh) + _dot_nt(qh, ml) + _dot_nt(ql, mh)
    col = lax.broadcasted_iota(jnp.int32, (rows, width), 1)
    row = lax.broadcasted_iota(jnp.int32, (rows, width), 0)
    valid = ((col & (HEADS - 1)) == (row & (HEADS - 1))) & (col < n_full * HEADS)
    gate = jnp.where(valid, gate, NEG)
    rank = jnp.zeros((rows, width), F32)
    for s in range(1, width // HEADS):
        other = pltpu.roll(gate, s * HEADS, axis=1)
        rank = rank + jnp.where(col >= s * HEADS, jnp.where(other >= gate, 1.0, 0.0), jnp.where(other > gate, 1.0, 0.0))
    blk = (col // HEADS).astype(F32)
    lane128 = lax.broadcasted_iota(jnp.int32, (rows, 128), 1)
    picks = jnp.zeros((rows, 128), F32)
    for k in range(MOBA_TOPK):
        pick = jnp.sum(jnp.where(valid & (rank == k), blk, 0.0), axis=-1, keepdims=True)
        picks = jnp.where(lane128 == k, pick, picks)
    idx_ref[...] = picks.astype(jnp.int32)

    c_new = jnp.concatenate([cnew_ref[...], jnp.zeros((PAGE_SIZE - dec, KV_LORA), F32)], axis=0).astype(BF16)
    s_new = scores(c_new, kpenew_ref[...].astype(BF16))
    qrow = lax.broadcasted_iota(jnp.int32, (rows, PAGE_SIZE), 0) // HEADS
    lane = lax.broadcasted_iota(jnp.int32, (rows, PAGE_SIZE), 1)
    s_scr[:, pl.ds(past, PAGE_SIZE)] = jnp.where(lane <= qrow, s_new, NEG)

    s_all = s_scr[...]
    m = jnp.max(s_all, axis=-1, keepdims=True)
    p_all = jnp.exp2(s_all - m)
    l = jnp.sum(p_all, axis=-1, keepdims=True)
    s_scr[...] = p_all

    def pv_tile(t, acc):
        t0 = pl.multiple_of(t * pv_tile_keys, pv_tile_keys)
        return acc + _dot(s_scr[:, pl.ds(t0, pv_tile_keys)].astype(BF16), cb16[pl.ds(t0, pv_tile_keys), :])

    acc = lax.fori_loop(0, past // pv_tile_keys, pv_tile, _dot(s_scr[:, pl.ds(past, PAGE_SIZE)].astype(BF16), c_new))
    o_ref[...] = acc / l


def _largest_tile(total, unit, cap):
    n = total // unit
    best = 1
    for d in range(1, n + 1):
        if n % d == 0 and d * unit <= cap:
            best = d
    return best * unit


def _decode(page_table, qabs, qpe, c_new, kpe_new_t, wkt, qm, lat, kpe_cache_t, k4):
    db, n_pages = page_table.shape
    dec = c_new.shape[1]
    rows = dec * HEADS
    past = n_pages * PAGE_SIZE
    tile = _largest_tile(past, MOBA_BLOCK, 1024)
    chunk_pages = tile // PAGE_SIZE
    n_full = n_pages // PAGES_PER_BLOCK
    width = -(-n_full * HEADS // 128) * 128
    unroll = math.gcd(n_pages, 8)
    assert db * (past // tile) >= SELECT_DEPTH - 1
    per_seq = lambda n: pl.BlockSpec((None, rows, n), lambda s, pt: (s, 0, 0))
    return pl.pallas_call(
        functools.partial(_decode_kernel, n_pages=n_pages, tile=tile, unroll=unroll,
                          pv_tile_keys=_largest_tile(past, PAGE_SIZE, 4096)),
        grid_spec=pltpu.PrefetchScalarGridSpec(
            num_scalar_prefetch=1,
            grid=(db,),
            in_specs=[per_seq(KV_LORA), per_seq(ROPE_DIM),
                      pl.BlockSpec((None, dec, KV_LORA), lambda s, pt: (s, 0, 0)),
                      pl.BlockSpec((None, ROPE_DIM, PAGE_SIZE), lambda s, pt: (s, 0, 0)),
                      pl.BlockSpec(wkt.shape, lambda s, pt: (0, 0)),
                      per_seq(HEAD_DIM),
                      pl.BlockSpec(memory_space=pl.ANY),
                      pl.BlockSpec(memory_space=pl.ANY),
                      pl.BlockSpec(memory_space=pl.ANY)],
            out_specs=[per_seq(KV_LORA), per_seq(128)],
            scratch_shapes=[pltpu.VMEM((2, past, KV_LORA), F32),
                            pltpu.VMEM((2, ROPE_DIM, past), F32),
                            pltpu.VMEM((past, KV_LORA), BF16),
                            pltpu.VMEM((rows, past + PAGE_SIZE), F32),
                            pltpu.VMEM((SELECT_DEPTH, chunk_pages, PAGE_SIZE, HEADS, HEAD_DIM), F32),
                            pltpu.VMEM((width, HEAD_DIM), F32),
                            pltpu.SemaphoreType.DMA((2, 2)),
                            pltpu.SemaphoreType.DMA((SELECT_DEPTH,))]),
        out_shape=[jax.ShapeDtypeStruct((db, rows, KV_LORA), F32), jax.ShapeDtypeStruct((db, rows, 128), jnp.int32)],
        compiler_params=_params(("arbitrary",)),
        name="decode_mla_attn_moba_select",
    )(page_table.reshape(-1), qabs, qpe, c_new, kpe_new_t, wkt, qm, lat, kpe_cache_t, k4)


def _mla_out_kernel(x_ref, wv_ref, o_ref):
    for h in range(HEADS):
        hi, lo = _split(x_ref[:, h * KV_LORA:(h + 1) * KV_LORA])
        wv = wv_ref[:, h * HEAD_DIM:(h + 1) * HEAD_DIM]
        o_ref[:, h * HEAD_DIM:(h + 1) * HEAD_DIM] = (_dot(hi, wv) + _dot(lo, wv)).astype(o_ref.dtype)


def _mla_out(x, wv):
    t = x.shape[0]
    return pl.pallas_call(
        _mla_out_kernel,
        grid=(1,),
        in_specs=[pl.BlockSpec(x.shape, lambda i: (0, 0)), pl.BlockSpec(wv.shape, lambda i: (0, 0))],
        out_specs=pl.BlockSpec((t, HEADS * HEAD_DIM), lambda i: (0, 0)),
        out_shape=jax.ShapeDtypeStruct((t, HEADS * HEAD_DIM), BF16),
        compiler_params=_params(("arbitrary",)),
        name="mla_sample_out",
    )(x, wv)


GATHER_DEPTH = 3


def _moba_sample_kernel(pt_ref, idx_ref, q_ref, kn_ref, vn_ref, last_ref, new_ref, k_hbm, v_hbm, o_ref,
                        kbuf, vbuf, sem, *, n_pages, dec):
    s_idx = pl.program_id(0)
    h = pl.program_id(1)
    g = s_idx * HEADS + h
    n_steps = pl.num_programs(0) * HEADS
    slot = g % GATHER_DEPTH
    n_full = n_pages // PAGES_PER_BLOCK
    n_sel = MOBA_TOPK * dec

    def copies(step, sl, q, k, j):
        seq = step // HEADS
        head = step % HEADS
        blk = idx_ref[step * n_sel + q * MOBA_TOPK + k]
        page = pt_ref[seq * n_pages + blk * PAGES_PER_BLOCK + j]
        dst = pl.ds((k * PAGES_PER_BLOCK + j) * PAGE_SIZE, PAGE_SIZE)
        return (pltpu.make_async_copy(k_hbm.at[page, :, head, :], kbuf.at[sl, q, dst], sem.at[0, sl]),
                pltpu.make_async_copy(v_hbm.at[page, :, head, :], vbuf.at[sl, q, dst], sem.at[1, sl]))

    def for_all(step, sl, fn):
        for q in range(dec):
            for k in range(MOBA_TOPK):
                for j in range(PAGES_PER_BLOCK):
                    for cp in copies(step, sl, q, k, j):
                        fn(cp)

    @pl.when(g == 0)
    def _():
        for d in range(GATHER_DEPTH - 1):
            for_all(d, d, lambda cp: cp.start())

    ahead = g + GATHER_DEPTH - 1

    @pl.when(ahead < n_steps)
    def _():
        for_all(ahead, ahead % GATHER_DEPTH, lambda cp: cp.start())

    for_all(g, slot, lambda cp: cp.wait())

    seg = MOBA_TOPK * MOBA_BLOCK
    total = dec * seg
    q8 = (q_ref[...] * (MOBA_SCALE * LOG2E)).astype(BF16)
    kn = kn_ref[...].astype(BF16)
    vn = vn_ref[...].astype(BF16)
    k_all = kbuf[slot].reshape(total, HEAD_DIM).astype(BF16)
    v_all = vbuf[slot].reshape(total, HEAD_DIM).astype(BF16)
    last = last_ref[...]
    bias = [jnp.where(idx_ref[g * n_sel + i] == n_full - 1, last, 0.0) for i in range(n_sel)]
    row = lax.broadcasted_iota(jnp.int32, (8, total), 0)
    col = lax.broadcasted_iota(jnp.int32, (8, total), 1)
    own = (col >= row * seg) & (col < (row + 1) * seg)
    s = jnp.where(own, _dot_nt(q8, k_all) + jnp.concatenate(bias, axis=-1), NEG)
    sn = _dot_nt(q8, kn) + new_ref[:, :8]
    m = jnp.maximum(jnp.max(s, axis=-1, keepdims=True), jnp.max(sn, axis=-1, keepdims=True))
    p = jnp.exp2(s - m)
    pn = jnp.exp2(sn - m)
    l = jnp.sum(p, axis=-1, keepdims=True) + jnp.sum(pn, axis=-1, keepdims=True)
    o = (_dot(p.astype(BF16), v_all) + _dot(pn.astype(BF16), vn)) / l
    o_ref[...] = o[:dec, :]


def _moba_sample(page_table, idx, q8, kn8, vn8, bias_last, bias_new, k4, v4, dec):
    db, n_pages = page_table.shape
    per = lambda: pl.BlockSpec((None, None, 8, HEAD_DIM), lambda s, h, pt, ix: (s, h, 0, 0))
    return pl.pallas_call(
        functools.partial(_moba_sample_kernel, n_pages=n_pages, dec=dec),
        grid_spec=pltpu.PrefetchScalarGridSpec(
            num_scalar_prefetch=2,
            grid=(db, HEADS),
            in_specs=[per(), per(), per(),
                      pl.BlockSpec((None, 8, MOBA_BLOCK), lambda s, h, pt, ix: (h, 0, 0)),
                      pl.BlockSpec((None, 8, 128), lambda s, h, pt, ix: (h, 0, 0)),
                      pl.BlockSpec(memory_space=pl.ANY),
                      pl.BlockSpec(memory_space=pl.ANY)],
            out_specs=pl.BlockSpec((None, dec, HEAD_DIM), lambda s, h, pt, ix: (s, 0, h)),
            scratch_shapes=[pltpu.VMEM((GATHER_DEPTH, dec, MOBA_TOPK * MOBA_BLOCK, HEAD_DIM), F32),
                            pltpu.VMEM((GATHER_DEPTH, dec, MOBA_TOPK * MOBA_BLOCK, HEAD_DIM), F32),
                            pltpu.SemaphoreType.DMA((2, GATHER_DEPTH))]),
        out_shape=jax.ShapeDtypeStruct((db, dec, HEADS * HEAD_DIM), F32),
        compiler_params=_params(("arbitrary", "arbitrary")),
        name="moba_sample_attn",
    )(page_table.reshape(-1), idx, q8, kn8, vn8, bias_last, bias_new, k4, v4)


def _merge_ffn_kernel(x_ref, oa_ref, ob_ref, ga_ref, wg_ref, wo_ref, gf_ref, wgu_ref, wd_ref, y_ref):
    x = x_ref[...]
    xn = _rms(x, ga_ref[...]).astype(BF16)
    g = _dot(xn, wg_ref[...])
    mix = jax.nn.sigmoid(g[:, :D_MODEL]) * oa_ref[...].astype(F32) + jax.nn.sigmoid(g[:, D_MODEL:]) * ob_ref[...].astype(F32)
    hres = x + _dot(mix.astype(BF16), wo_ref[...])
    gu = _dot(_rms(hres, gf_ref[...]).astype(BF16), wgu_ref[...])
    d_ff = wd_ref.shape[0]
    act = jax.nn.silu(gu[:, :d_ff]) * gu[:, d_ff:]
    y_ref[...] = hres + _dot(act.astype(BF16), wd_ref[...])


def _merge_ffn(x, o_a, o_b, w):
    t = x.shape[0]
    tm = ROW_TILE
    row = lambda: pl.BlockSpec((tm, D_MODEL), lambda i: (i, 0))
    consts = [w["norm_attn"], w["w_g"], w["w_o"], w["norm_ffn"], w["w_gate_up"], w["w_down"]]
    return pl.pallas_call(
        _merge_ffn_kernel,
        grid=(t // tm,),
        in_specs=[row(), row(), row()] + [_const_spec(a.shape) for a in consts],
        out_specs=row(),
        out_shape=jax.ShapeDtypeStruct((t, D_MODEL), F32),
        compiler_params=_params(("parallel",)),
        name="merge_ffn",
    )(x, o_a, o_b, *consts)


def _rope_tables(pos):
    half = ROPE_DIM // 2
    freqs = ROPE_THETA ** (-jnp.arange(half, dtype=F32) / half)
    ang = pos.astype(F32)[:, None] * freqs
    cos, sin = jnp.cos(ang), jnp.sin(ang)
    return jnp.concatenate([cos, cos], axis=-1), jnp.concatenate([-sin, sin], axis=-1)


def _layer_weights(l, norm_attn, w_in, q_a_norm, w_q_b, kv_a_norm, w_kv_b, mla_qn_gain, mla_qr_gain, mla_kn_gain,
                   mla_kr_gain, moba_q_gain, moba_k_gain, w_o, norm_ffn, w_gate_up, w_down):
    width = HEADS * HEAD_DIM
    a_end = Q_LORA + KV_LORA + ROPE_DIM
    row = lambda a: a[l].reshape(1, -1)
    win = w_in[l].astype(BF16)
    wqb = w_q_b[l].reshape(Q_LORA, HEADS, HEAD_DIM + ROPE_DIM)
    wkv = w_kv_b[l].reshape(KV_LORA, HEADS, 2 * HEAD_DIM)
    return {
        "norm_attn": row(norm_attn), "q_a_norm": row(q_a_norm), "kv_a_norm": row(kv_a_norm),
        "qn_gain": row(mla_qn_gain), "qr_gain": row(mla_qr_gain), "kn_gain": row(mla_kn_gain),
        "kr_gain": row(mla_kr_gain), "moba_q_gain": row(moba_q_gain), "moba_k_gain": row(moba_k_gain),
        "norm_ffn": row(norm_ffn),
        "w_a": win[:, :a_end], "w_m": win[:, a_end:a_end + 3 * width], "w_g": win[:, a_end + 3 * width:],
        "w_q_b": jnp.concatenate([wqb[:, :, :HEAD_DIM].reshape(Q_LORA, -1),
                                  wqb[:, :, HEAD_DIM:].reshape(Q_LORA, -1)], axis=-1).astype(BF16),
        "w_kv": jnp.concatenate([wkv[:, :, :HEAD_DIM].reshape(KV_LORA, -1),
                                 wkv[:, :, HEAD_DIM:].reshape(KV_LORA, -1)], axis=-1).astype(BF16),
        "w_o": w_o[l].astype(BF16), "w_gate_up": w_gate_up[l].astype(BF16), "w_down": w_down[l].astype(BF16),
    }


def kernel(x_prompt, x_sample, cache_mla_latent, cache_mla_kpe, cache_moba_k, cache_moba_v, page_table, norm_attn, w_in, q_a_norm, w_q_b, kv_a_norm, w_kv_b, mla_qn_gain, mla_qr_gain, mla_kn_gain, mla_kr_gain, moba_q_gain, moba_k_gain, t5_table, w_o, norm_ffn, w_gate_up, w_down):
    batch, seq, _ = x_prompt.shape
    db, dec, _ = x_sample.shape
    depth = w_in.shape[0]
    n_pool = cache_mla_latent.shape[1]
    n_pages = page_table.shape[1]
    past = n_pages * PAGE_SIZE
    width = HEADS * HEAD_DIM
    assert seq % ROW_TILE == 0 and (db * dec) % ROW_TILE == 0 and ROW_TILE % dec == 0
    assert past % MOBA_BLOCK == 0 and past // MOBA_BLOCK >= MOBA_TOPK and dec <= 8

    cos_p, sin_p = _rope_tables(jnp.arange(seq))
    cos_s, sin_s = _rope_tables(past + jnp.arange(ROW_TILE) % dec)
    tiles, bias_last, bias_new = _t5_tables(t5_table, 8)

    yp = x_prompt.reshape(batch * seq, D_MODEL)
    ys = x_sample.reshape(db * dec, D_MODEL)
    outs = [[] for _ in range(8)]
    for l in range(depth):
        w = _layer_weights(l, norm_attn, w_in, q_a_norm, w_q_b, kv_a_norm, w_kv_b, mla_qn_gain, mla_qr_gain,
                           mla_kn_gain, mla_kr_gain, moba_q_gain, moba_k_gain, w_o, norm_ffn, w_gate_up, w_down)
        c, kpe, qcat, kcat, v = _mla_proj(yp, w, cos_p, sin_p, sample=False)
        qm, km, vm, kext, vb, means = _moba_proj(yp, w, blocks_per_seq=seq // MOBA_BLOCK)
        o_a = _mla_prompt(qcat, kcat, v, batch, seq)
        o_b = _moba_prompt(qm, kext, vb, means.reshape(batch, seq // MOBA_BLOCK, width), tiles, batch, seq)
        outs[0].append(c.reshape(batch, seq, KV_LORA))
        outs[1].append(kpe.reshape(batch, seq, ROPE_DIM))
        outs[2].append(km.reshape(batch, seq, HEADS, HEAD_DIM))
        outs[3].append(vm.reshape(batch, seq, HEADS, HEAD_DIM))
        yp = _merge_ffn(yp, o_a, o_b, w)
        c, kpe, qabs, qpe = _mla_proj(ys, w, cos_s, sin_s, sample=True)
        qm, km, vm = _moba_proj(ys, w)
        kpe_new_t = jnp.pad(jnp.swapaxes(kpe.reshape(db, dec, ROPE_DIM), 1, 2), ((0, 0), (0, 0), (0, PAGE_SIZE - dec)))
        k4 = cache_moba_k[l]
        v4 = cache_moba_v[l]
        lat, idx = _decode(page_table, qabs.reshape(db, dec * HEADS, KV_LORA), qpe.reshape(db, dec * HEADS, ROPE_DIM),
                           c.reshape(db, dec, KV_LORA), kpe_new_t, w["w_kv"][:, :width].T,
                           qm.reshape(db, dec * HEADS, HEAD_DIM),
                           cache_mla_latent[l], jnp.swapaxes(cache_mla_kpe[l], 1, 2), k4)
        o_a = _mla_out(lat.reshape(db * dec, HEADS * KV_LORA), w["w_kv"][:, width:])
        idx = jnp.transpose(idx[:, :, :MOBA_TOPK].reshape(db, dec, HEADS, MOBA_TOPK), (0, 2, 1, 3)).reshape(-1)
        to_heads = lambda a: jnp.pad(jnp.transpose(a.reshape(db, dec, HEADS, HEAD_DIM), (0, 2, 1, 3)),
                                     ((0, 0), (0, 0), (0, 8 - dec), (0, 0)))
        o_b = _moba_sample(page_table, idx, to_heads(qm), to_heads(km), to_heads(vm), bias_last, bias_new, k4, v4, dec)
        outs[4].append(c.reshape(db, dec, KV_LORA))
        outs[5].append(kpe.reshape(db, dec, ROPE_DIM))
        outs[6].append(km.reshape(db, dec, HEADS, HEAD_DIM))
        outs[7].append(vm.reshape(db, dec, HEADS, HEAD_DIM))
        ys = _merge_ffn(ys, o_a, o_b.reshape(db * dec, width), w)
    return (yp.reshape(batch, seq, D_MODEL), ys.reshape(db, dec, D_MODEL)) + tuple(jnp.stack(o, axis=0) for o in outs)
```
